```python
import math
import jax
import jax.numpy as jnp
from jax import lax
import numpy as np

D_MODEL = 1024
BATCH = 16
SEQ = 256
DEPTH = 4
DEC_BATCH = 8
DEC_SEQ = 4096
PAST_LEN = 512

GRID_W = 64
HEAD_DIM = 64
A_HEADS = 4
A_QK_DIM = 32
B_HEADS = 6
B_KV_HEADS = 2
C_HEADS = 6
C_KV_HEADS = 2
WINDOW = 128
Q_BLOCK = 128
ROPE_THETA = 10000.0
MIX_WIDTH = (A_HEADS + B_HEADS + C_HEADS) * HEAD_DIM
IN_WIDTHS = (A_HEADS * 2 * A_QK_DIM, A_HEADS * 2 * A_QK_DIM, A_HEADS * HEAD_DIM,
             B_HEADS * HEAD_DIM, B_KV_HEADS * HEAD_DIM, B_KV_HEADS * HEAD_DIM,
             C_HEADS * HEAD_DIM, C_KV_HEADS * HEAD_DIM, C_KV_HEADS * HEAD_DIM)
IN_WIDTH = sum(IN_WIDTHS)
N_EXPERTS = 32
TOP_K = 4
D_FF = D_MODEL
SWIGLU_ALPHA = 1.702
SWIGLU_LIMIT = 7.0
MOE_BLOCK = 256
LN_EPS = 1e-5
RMS_EPS = 1e-6
DEEPNORM_ALPHA = (2 * DEPTH) ** 0.25
DEEPNORM_BETA = (8 * DEPTH) ** -0.25
NEG_INF = -1e30

kernel_name = 'hybrid_flow_trunk_step'


def layer_norm(x, g, b):
    xf = x.astype(jnp.float32)
    mu = jnp.mean(xf, -1, keepdims=True)
    var = jnp.mean(jnp.square(xf - mu), -1, keepdims=True)
    return ((xf - mu) * lax.rsqrt(var + LN_EPS) * g + b).astype(x.dtype)


def rms_norm(x, g):
    xf = x.astype(jnp.float32)
    return (xf * lax.rsqrt(jnp.mean(xf * xf, -1, keepdims=True) + RMS_EPS) * g).astype(x.dtype)


def lambda_init(layer):
    return 0.8 - 0.6 * math.exp(-0.3 * layer)


def rope_tables(rows, dim):
    row = jnp.repeat(jnp.arange(rows), GRID_W).astype(jnp.float32)
    col = jnp.tile(jnp.arange(GRID_W), rows).astype(jnp.float32)
    nf = dim // 4
    freqs = ROPE_THETA ** (-jnp.arange(nf, dtype=jnp.float32) / nf)
    ang = jnp.concatenate([row[:, None] * freqs, col[:, None] * freqs], -1)
    return jnp.cos(ang), jnp.sin(ang)


def apply_rope_2d(x, cos, sin):
    q = x.shape[-1] // 4
    xf = x.astype(jnp.float32)
    parts = []
    for axis in range(2):
        xa = xf[..., axis * 2 * q:(axis + 1) * 2 * q]
        x1, x2 = xa[..., :q], xa[..., q:]
        c = cos[:, None, axis * q:(axis + 1) * q]
        s = sin[:, None, axis * q:(axis + 1) * q]
        parts += [x1 * c - x2 * s, x1 * s + x2 * c]
    return jnp.concatenate(parts, -1).astype(x.dtype)


def rope_diff(x, cos, sin):
    b, s, h, d = x.shape
    return apply_rope_2d(x.reshape(b, s, 2 * h, d // 2), cos, sin).reshape(b, s, h, d)


def map_query_blocks(fn, q):
    b, s = q.shape[:2]
    nb = s // Q_BLOCK
    blocks = jnp.moveaxis(q.reshape(b, nb, Q_BLOCK, *q.shape[2:]), 1, 0)
    out = lax.map(lambda args: fn(args[0], args[1]), (jnp.arange(nb), blocks))
    return jnp.moveaxis(out, 0, 1).reshape(b, s, *out.shape[3:])


def gqa_attend(q, k, v, sink=None, mask=None):
    b, nq, hq, d = q.shape
    hk = k.shape[2]
    g = hq // hk
    qg = q.reshape(b, nq, hk, g, d)
    s = jnp.einsum('bqhgd,bkhd->bhgqk', qg, k, preferred_element_type=jnp.float32) * (d ** -0.5)
    if mask is not None:
        s = jnp.where(mask, s, NEG_INF)
    if sink is not None:
        sink_col = jnp.broadcast_to(sink.astype(jnp.float32).reshape(1, hk, g, 1, 1), s.shape[:-1] + (1,))
        p = jax.nn.softmax(jnp.concatenate([s, sink_col], -1), axis=-1)[..., :-1]
    else:
        p = jax.nn.softmax(s, axis=-1)
    o = jnp.einsum('bhgqk,bkhd->bqhgd', p.astype(v.dtype), v)
    return o.reshape(b, nq, hq, v.shape[-1])


def diff_attend(q, k, v, lam):
    q1, q2 = jnp.split(q, 2, axis=-1)
    k1, k2 = jnp.split(k, 2, axis=-1)
    scale = A_QK_DIM ** -0.5
    s1 = jnp.einsum('bqhd,bkhd->bhqk', q1, k1, preferred_element_type=jnp.float32) * scale
    s2 = jnp.einsum('bqhd,bkhd->bhqk', q2, k2, preferred_element_type=jnp.float32) * scale
    p = jax.nn.softmax(s1, axis=-1) - lam * jax.nn.softmax(s2, axis=-1)
    return jnp.einsum('bhqk,bkhd->bqhd', p.astype(v.dtype), v)


def window_attend_latent(q, k, v, k_ctx, v_ctx, sink):
    s = q.shape[1]
    pad = ((0, 0), (WINDOW, WINDOW), (0, 0), (0, 0))
    kp = jnp.pad(k, pad)
    vp = jnp.pad(v, pad)
    span = Q_BLOCK + 2 * WINDOW
    qi = jnp.arange(Q_BLOCK)
    kj = jnp.arange(span) - WINDOW
    ctx_mask = jnp.ones((Q_BLOCK, k_ctx.shape[1]), dtype=bool)

    def block(i, qb):
        kb = lax.dynamic_slice_in_dim(kp, i * Q_BLOCK, span, axis=1)
        vb = lax.dynamic_slice_in_dim(vp, i * Q_BLOCK, span, axis=1)
        qpos = i * Q_BLOCK + qi
        kpos = i * Q_BLOCK + kj
        band = (jnp.abs(qpos[:, None] - kpos[None, :]) <= WINDOW) & (kpos[None, :] >= 0) & (kpos[None, :] < s)
        mask = jnp.concatenate([band, ctx_mask], axis=-1)
        return gqa_attend(qb, jnp.concatenate([kb, k_ctx], 1), jnp.concatenate([vb, v_ctx], 1), sink, mask)

    return map_query_blocks(block, q)


def clamped_swiglu(gu):
    glu, lin = jnp.split(gu, 2, axis=-1)
    glu = jnp.minimum(glu, SWIGLU_LIMIT)
    lin = jnp.clip(lin, -SWIGLU_LIMIT, SWIGLU_LIMIT)
    return glu * jax.nn.sigmoid(SWIGLU_ALPHA * glu) * (lin + 1)


def moe_ffn(h, p):
    b, s, d = h.shape
    n_tok = b * s
    t = h.reshape(n_tok, d)
    logits = jnp.dot(t, p['w_router'], preferred_element_type=jnp.float32) + p['b_router'].astype(jnp.float32)
    top_val, top_idx = lax.top_k(logits, TOP_K)
    gates = jax.nn.softmax(top_val, axis=-1)
    n_asg = n_tok * TOP_K
    flat_e = top_idx.reshape(-1)
    order = jnp.argsort(flat_e)
    sorted_e = flat_e[order]
    counts = jnp.bincount(flat_e, length=N_EXPERTS)
    starts = jnp.cumsum(counts) - counts
    padded = (counts + MOE_BLOCK - 1) // MOE_BLOCK * MOE_BLOCK
    pends = jnp.cumsum(padded)
    pstarts = pends - padded
    dest_sorted = (pstarts[sorted_e] + jnp.arange(n_asg) - starts[sorted_e]).astype(jnp.int32)
    dest = jnp.zeros((n_asg,), jnp.int32).at[order].set(dest_sorted)
    n_blocks = -(-n_asg // MOE_BLOCK) + N_EXPERTS
    buf = jnp.zeros((n_blocks * MOE_BLOCK, d), h.dtype).at[dest].set(t[jnp.arange(n_asg) // TOP_K])
    block_e = jnp.minimum(jnp.searchsorted(pends, jnp.arange(n_blocks) * MOE_BLOCK, side='right'), N_EXPERTS - 1)

    def expert_block(args):
        xb, e = args
        gu = xb @ p['w_gate_up'][e] + p['b_gate_up'][e]
        return clamped_swiglu(gu) @ p['w_down'][e] + p['b_down'][e]

    out = lax.map(expert_block, (buf.reshape(n_blocks, MOE_BLOCK, d), block_e)).reshape(-1, d)
    y = jnp.sum(out[dest].reshape(n_tok, TOP_K, d) * gates[..., None].astype(out.dtype), axis=1)
    return y.astype(h.dtype).reshape(b, s, d)


def split_points():
    pts, acc = [], 0
    for w in IN_WIDTHS[:-1]:
        acc += w
        pts.append(acc)
    return pts


def trunk_layer(x, cond, p, lam_init, ctx_kv=None, rope=None):
    bsz, s = x.shape[:2]
    mod = jax.nn.silu(cond) @ p['w_mod'] + p['b_mod']
    shift1, scale1, gate1, shift2, scale2, gate2 = [m[:, None, :] for m in jnp.split(mod, 6, axis=-1)]
    h = x * (1 + scale1) + shift1
    proj = h @ p['w_in']
    qa, ka, va, qb, kb, vb, qc, kc, vc = jnp.split(proj, split_points(), axis=-1)
    heads = lambda a, n: a.reshape(bsz, s, n, -1)
    qa, ka, va = heads(qa, A_HEADS), heads(ka, A_HEADS), heads(va, A_HEADS)
    qb = rms_norm(heads(qb, B_HEADS), p['b_q_norm_g'])
    kb = rms_norm(heads(kb, B_KV_HEADS), p['b_k_norm_g'])
    vb = heads(vb, B_KV_HEADS)
    qc, kc, vc = heads(qc, C_HEADS), heads(kc, C_KV_HEADS), heads(vc, C_KV_HEADS)
    lam_vec = p['a_lambda'].astype(jnp.float32)
    lam = jnp.exp(jnp.sum(lam_vec[0] * lam_vec[1])) - jnp.exp(jnp.sum(lam_vec[2] * lam_vec[3])) + lam_init
    if ctx_kv is None:
        new_kv = (ka, va, kb, vb, kc, vc)
        oa = map_query_blocks(lambda i, q: diff_attend(q, ka, va, lam), qa)
        ob = map_query_blocks(lambda i, q: gqa_attend(q, kb, vb), qb)
        oc = map_query_blocks(lambda i, q: gqa_attend(q, kc, vc, sink=p['c_sink']), qc)
    else:
        new_kv = None
        ka_x, va_x, kb_x, vb_x, kc_x, vc_x = ctx_kv
        (cos64, sin64), (cos32, sin32) = rope
        qa, ka = rope_diff(qa, cos32, sin32), rope_diff(ka, cos32, sin32)
        qb, kb = apply_rope_2d(qb, cos64, sin64), apply_rope_2d(kb, cos64, sin64)
        qc, kc = apply_rope_2d(qc, cos64, sin64), apply_rope_2d(kc, cos64, sin64)
        ka_all, va_all = jnp.concatenate([ka, ka_x], 1), jnp.concatenate([va, va_x], 1)
        kb_all, vb_all = jnp.concatenate([kb, kb_x], 1), jnp.concatenate([vb, vb_x], 1)
        oa = map_query_blocks(lambda i, q: diff_attend(q, ka_all, va_all, lam), qa)
        ob = map_query_blocks(lambda i, q: gqa_attend(q, kb_all, vb_all), qb)
        oc = window_attend_latent(qc, kc, vc, kc_x, vc_x, p['c_sink'])
    oa = rms_norm(oa, p['a_subln_g']) * (1.0 - lam_init)
    mixed = jnp.concatenate([oa.reshape(bsz, s, -1), ob.reshape(bsz, s, -1), oc.reshape(bsz, s, -1)], axis=-1)
    o = mixed @ p['w_out']
    x = layer_norm(DEEPNORM_ALPHA * x + gate1 * o, p['ln1_g'], p['ln1_b'])
    h2 = x * (1 + scale2) + shift2
    x = layer_norm(DEEPNORM_ALPHA * x + gate2 * moe_ffn(h2, p), p['ln2_g'], p['ln2_b'])
    return x, new_kv


def setup_inputs(seed: int = 0) -> dict:
    key = jax.random.key(seed)
    ks = jax.random.split(key, 32)
    L = DEPTH

    def nrm(k, shape, scale):
        return jax.random.normal(k, shape, jnp.float32) * scale

    return {
        'x_prompt': nrm(ks[0], (BATCH, SEQ, D_MODEL), 1.0),
        'x_sample': nrm(ks[1], (DEC_BATCH, DEC_SEQ, D_MODEL), 1.0),
        'cache_a_k': nrm(ks[2], (DEC_BATCH, L, PAST_LEN, A_HEADS, 2 * A_QK_DIM), 1.0),
        'cache_a_v': nrm(ks[3], (DEC_BATCH, L, PAST_LEN, A_HEADS, HEAD_DIM), 1.0),
        'cache_b_k': nrm(ks[4], (DEC_BATCH, L, PAST_LEN, B_KV_HEADS, HEAD_DIM), 1.0),
        'cache_b_v': nrm(ks[5], (DEC_BATCH, L, PAST_LEN, B_KV_HEADS, HEAD_DIM), 1.0),
        'cache_c_k': nrm(ks[6], (DEC_BATCH, L, PAST_LEN, C_KV_HEADS, HEAD_DIM), 1.0),
        'cache_c_v': nrm(ks[7], (DEC_BATCH, L, PAST_LEN, C_KV_HEADS, HEAD_DIM), 1.0),
        'c': nrm(ks[8], (DEC_BATCH, D_MODEL), 1.0),
        'c_ctx': nrm(ks[9], (D_MODEL,), 1.0),
        'w_mod': nrm(ks[10], (L, D_MODEL, 6 * D_MODEL), 0.5 * D_MODEL ** -0.5),
        'b_mod': nrm(ks[11], (L, 6 * D_MODEL), 0.02),
        'w_in': nrm(ks[12], (L, D_MODEL, IN_WIDTH), D_MODEL ** -0.5),
        'a_lambda': nrm(ks[13], (L, 4, A_QK_DIM), 0.1),
        'a_subln_g': 1.0 + nrm(ks[14], (L, HEAD_DIM), 0.02),
        'b_q_norm_g': 1.0 + nrm(ks[15], (L, HEAD_DIM), 0.02),
        'b_k_norm_g': 1.0 + nrm(ks[16], (L, HEAD_DIM), 0.02),
        'c_sink': nrm(ks[17], (L, C_HEADS), 0.5),
        'w_out': nrm(ks[18], (L, MIX_WIDTH, D_MODEL), DEEPNORM_BETA * MIX_WIDTH ** -0.5),
        'ln1_g': 1.0 + nrm(ks[19], (L, D_MODEL), 0.02),
        'ln1_b': nrm(ks[20], (L, D_MODEL), 0.02),
        'w_router': nrm(ks[21], (L, D_MODEL, N_EXPERTS), D_MODEL ** -0.5),
        'b_router': nrm(ks[22], (L, N_EXPERTS), 0.01),
        'w_gate_up': nrm(ks[23], (L, N_EXPERTS, D_MODEL, 2 * D_FF), D_MODEL ** -0.5),
        'b_gate_up': nrm(ks[24], (L, N_EXPERTS, 2 * D_FF), 0.02),
        'w_down': nrm(ks[25], (L, N_EXPERTS, D_FF, D_MODEL), DEEPNORM_BETA * D_FF ** -0.5),
        'b_down': nrm(ks[26], (L, N_EXPERTS, D_MODEL), 0.02),
        'ln2_g': 1.0 + nrm(ks[27], (L, D_MODEL), 0.02),
        'ln2_b': nrm(ks[28], (L, D_MODEL), 0.02),
    }


def reference(x_prompt, x_sample, cache_a_k, cache_a_v, cache_b_k, cache_b_v, cache_c_k, cache_c_v,
              c, c_ctx, w_mod, b_mod, w_in, a_lambda, a_subln_g, b_q_norm_g, b_k_norm_g, c_sink,
              w_out, ln1_g, ln1_b, w_router, b_router, w_gate_up, b_gate_up, w_down, b_down,
              ln2_g, ln2_b):
    layers = [dict(w_mod=w_mod[l], b_mod=b_mod[l], w_in=w_in[l], a_lambda=a_lambda[l],
                   a_subln_g=a_subln_g[l], b_q_norm_g=b_q_norm_g[l], b_k_norm_g=b_k_norm_g[l],
                   c_sink=c_sink[l], w_out=w_out[l], ln1_g=ln1_g[l], ln1_b=ln1_b[l],
                   w_router=w_router[l], b_router=b_router[l], w_gate_up=w_gate_up[l],
                   b_gate_up=b_gate_up[l], w_down=w_down[l], b_down=b_down[l],
                   ln2_g=ln2_g[l], ln2_b=ln2_b[l]) for l in range(DEPTH)]

    y = x_prompt
    kv_layers = []
    for l in range(DEPTH):
        y, kv = trunk_layer(y, c_ctx[None, :], layers[l], lambda_init(l))
        kv_layers.append(kv)
    new_a_k = jnp.stack([kv[0] for kv in kv_layers], axis=1)
    new_a_v = jnp.stack([kv[1] for kv in kv_layers], axis=1)
    new_b_k = jnp.stack([kv[2] for kv in kv_layers], axis=1)
    new_b_v = jnp.stack([kv[3] for kv in kv_layers], axis=1)
    new_c_k = jnp.stack([kv[4] for kv in kv_layers], axis=1)
    new_c_v = jnp.stack([kv[5] for kv in kv_layers], axis=1)

    rows = x_sample.shape[1] // GRID_W
    rope = (rope_tables(rows, HEAD_DIM), rope_tables(rows, A_QK_DIM))
    z = x_sample
    for l in range(DEPTH):
        ctx_kv = (cache_a_k[:, l], cache_a_v[:, l], cache_b_k[:, l], cache_b_v[:, l],
                  cache_c_k[:, l], cache_c_v[:, l])
        z, _ = trunk_layer(z, c, layers[l], lambda_init(l), ctx_kv=ctx_kv, rope=rope)

    return (y, z, new_a_k, new_a_v, new_b_k, new_b_v, new_c_k, new_c_v)
```

```python
import functools
import math

import jax
import jax.numpy as jnp
from jax import lax
from jax.experimental import pallas as pl
from jax.experimental.pallas import tpu as pltpu

F32 = jnp.float32
BF16 = jnp.bfloat16

LANE = 128
HEAD_DIM = 64
A_HEADS = 4
A_QK_DIM = 32
B_HEADS = 6
B_KV_HEADS = 2
C_HEADS = 6
C_KV_HEADS = 2
GRID_W = 64
WINDOW = 128
ROPE_THETA = 10000.0
N_EXPERTS = 32
TOP_K = 4
SWIGLU_ALPHA = 1.702
SWIGLU_LIMIT = 7.0
MOE_BLOCK = 256
LN_EPS = 1e-5
RMS_EPS = 1e-6
NEG = -1e30
LOG2E = math.log2(math.e)
VMEM_LIMIT = 56 * 1024 * 1024

QB_T, QC_T, QA_T, KA_T, KB_T, KC_T, VA_T, VB_T, VC_T = 0, 3, 6, 8, 10, 11, 12, 14, 15
GQA_ORDER = (0, 3, 1, 4, 2, 5)


def _lambda_init(layer):
    return 0.8 - 0.6 * math.exp(-0.3 * layer)


def _lane_iota(shape):
    return lax.broadcasted_iota(jnp.int32, shape, len(shape) - 1)


def _cparams(sem, vmem=VMEM_LIMIT):
    return pltpu.CompilerParams(dimension_semantics=sem, vmem_limit_bytes=vmem)


def _mod_kernel(c_ref, w_ref, b_ref, o_ref):
    c = c_ref[...]
    s = (c * jax.nn.sigmoid(c)).astype(BF16)
    o_ref[...] = jnp.dot(s, w_ref[...].astype(BF16), preferred_element_type=F32) + b_ref[...]


def _modulation(cond, w_mod, b_mod):
    depth, d, n = w_mod.shape
    g = cond.shape[0]
    tn = 1536
    return pl.pallas_call(
        _mod_kernel,
        grid=(depth, n // tn),
        in_specs=[pl.BlockSpec((g, d), lambda l, j: (0, 0)),
                  pl.BlockSpec((None, d, tn), lambda l, j: (l, 0, j)),
                  pl.BlockSpec((None, 1, tn), lambda l, j: (l, 0, j))],
        out_specs=pl.BlockSpec((None, g, tn), lambda l, j: (l, 0, j)),
        out_shape=jax.ShapeDtypeStruct((depth, g, n), F32),
        compiler_params=_cparams(("arbitrary", "arbitrary")),
        name="modulation",
    )(cond, w_mod, b_mod.reshape(depth, 1, n))


def _swap_blocks(x, blk, lane):
    up = pltpu.roll(x, LANE - blk, 1)
    dn = pltpu.roll(x, blk, 1)
    return jnp.where((lane % (2 * blk)) < blk, up, dn)


def _segment_mean_sq(x, ones_seg):
    sq = x * x
    hi = sq.astype(BF16)
    lo = (sq - hi.astype(F32)).astype(BF16)
    tot = (jnp.dot(hi, ones_seg, preferred_element_type=F32)
           + jnp.dot(lo, ones_seg, preferred_element_type=F32))
    return tot * (1.0 / HEAD_DIM)


def _inproj_kernel(x_ref, mod_ref, w_ref, rope_ref, g_ref, proj_ref, kv_ref, *, n_ctx_tiles):
    i = pl.program_id(0)
    tm = x_ref.shape[0]
    x = x_ref[...]
    shift1 = mod_ref[0:1, :]
    scale1 = mod_ref[1:2, :]
    h = (x * (1.0 + scale1) + shift1).astype(BF16)

    lane = _lane_iota((tm, LANE))
    r_i = lax.broadcasted_iota(jnp.int32, (LANE, LANE), 0)
    c_i = lax.broadcasted_iota(jnp.int32, (LANE, LANE), 1)
    ones_seg = jnp.where((r_i // HEAD_DIM) == (c_i // HEAD_DIM), 1.0, 0.0).astype(BF16)

    cos_a, sin_a, cos_b, sin_b = rope_ref[0], rope_ref[1], rope_ref[2], rope_ref[3]
    gq = g_ref[0:1, :]
    gk = g_ref[1:2, :]
    qa_scale = (A_QK_DIM ** -0.5) * LOG2E
    q_scale = (HEAD_DIM ** -0.5) * LOG2E

    def section(t0, nt):
        return jnp.dot(h, w_ref[:, t0 * LANE:(t0 + nt) * LANE], preferred_element_type=F32)

    def tile(sec, t):
        return sec[:, t * LANE:(t + 1) * LANE]

    def rope(xt, cos, sin, blk):
        return xt * cos + _swap_blocks(xt, blk, lane) * sin

    def rms(xt, g):
        return xt * lax.rsqrt(_segment_mean_sq(xt, ones_seg) + RMS_EPS) * g

    def put(t, val):
        proj_ref[:, t * LANE:(t + 1) * LANE] = val.astype(BF16)

    kv_tiles = []

    sec = section(QB_T, 3)
    for t in range(3):
        put(QB_T + t, rope(rms(tile(sec, t), gq), cos_b, sin_b, 16) * q_scale)
    sec = section(QC_T, 3)
    for t in range(3):
        put(QC_T + t, rope(tile(sec, t), cos_b, sin_b, 16) * q_scale)
    sec = section(QA_T, 4)
    for t in range(2):
        put(QA_T + t, rope(tile(sec, t), cos_a, sin_a, 8) * qa_scale)
    for t in range(2):
        ka = rope(tile(sec, 2 + t), cos_a, sin_a, 8)
        put(KA_T + t, ka)
        kv_tiles.append(ka)
    sec = section(KB_T, 2)
    kb = rope(rms(tile(sec, 0), gk), cos_b, sin_b, 16)
    put(KB_T, kb)
    kc = rope(tile(sec, 1), cos_b, sin_b, 16)
    put(KC_T, kc)
    kv_tiles += [kb, kc]
    sec = section(VA_T, 4)
    for t in range(4):
        put(VA_T + t, tile(sec, t))
        kv_tiles.append(tile(sec, t))

    @pl.when(i < n_ctx_tiles)
    def _():
        for t, val in enumerate(kv_tiles):
            kv_ref[:, t * LANE:(t + 1) * LANE] = val


def _inproj(x, mod_l, w_in_p, rope_tab, gains, *, tm, t_ctx, s_dec):
    t_all, d = x.shape
    n_ctx_tiles = t_ctx // tm
    tiles_per_seq = s_dec // tm
    s_rope_tiles = (rope_tab.shape[1] - tm) // tm

    def grp(i):
        return jnp.where(i < n_ctx_tiles, 0, 1 + (i - n_ctx_tiles) // tiles_per_seq)

    def rope_blk(i):
        return jnp.where(i < n_ctx_tiles, s_rope_tiles, (i - n_ctx_tiles) % tiles_per_seq)

    n_out = w_in_p.shape[1]
    return pl.pallas_call(
        functools.partial(_inproj_kernel, n_ctx_tiles=n_ctx_tiles),
        grid=(t_all // tm,),
        in_specs=[pl.BlockSpec((tm, d), lambda i: (i, 0)),
                  pl.BlockSpec((None, 6, d), lambda i: (grp(i), 0, 0)),
                  pl.BlockSpec((d, n_out), lambda i: (0, 0)),
                  pl.BlockSpec((4, tm, LANE), lambda i: (0, rope_blk(i), 0)),
                  pl.BlockSpec((8, LANE), lambda i: (0, 0))],
        out_specs=[pl.BlockSpec((tm, n_out), lambda i: (i, 0)),
                   pl.BlockSpec((tm, 8 * LANE), lambda i: (jnp.minimum(i, n_ctx_tiles - 1), 0))],
        out_shape=[jax.ShapeDtypeStruct((t_all, n_out), BF16),
                   jax.ShapeDtypeStruct((t_ctx, 8 * LANE), F32)],
        compiler_params=_cparams(("arbitrary",)),
        name="inproj",
    )(x, mod_l, w_in_p, rope_tab, gains)


def _attn_full_kernel(*refs, mode, nt, tq, ck_new, n_new, ck_ctx, n_ctx, has_sink, lam_init):
    refs = list(refs)
    sink_ref = refs.pop(0) if has_sink else None
    q_ref, k_ref, v_ref = refs[:3]
    refs = refs[3:]
    if n_ctx:
        kx_ref, vx_ref = refs[:2]
        refs = refs[2:]
    if mode == "diff":
        alam_ref, g_ref = refs[:2]
        refs = refs[2:]
    o_ref, qs_ref, m_ref, l_ref, acc_ref = refs

    n_grp = 4 if mode == "diff" else 2
    gw = LANE // n_grp
    rows = nt * n_grp * tq
    lane = _lane_iota((tq, LANE))

    for t in range(nt):
        qt = q_ref[:, t * LANE:(t + 1) * LANE]
        for r in range(n_grp):
            row0 = (t * n_grp + r) * tq
            qs_ref[row0:row0 + tq, :] = jnp.where((lane // gw) == r, qt, jnp.zeros_like(qt))

    if has_sink:
        for t in range(nt):
            for r in range(n_grp):
                row0 = (t * n_grp + r) * tq
                sk = sink_ref[t + 3 * r] * LOG2E
                m_ref[row0:row0 + tq, :] = jnp.full((tq, 1), 0.0, F32) + sk
        l_ref[...] = jnp.ones((rows, 1), F32)
    else:
        m_ref[...] = jnp.full((rows, 1), NEG, F32)
        l_ref[...] = jnp.zeros((rows, 1), F32)
    acc_ref[...] = jnp.zeros((rows, LANE), F32)

    def step(kc, vc):
        s = lax.dot_general(qs_ref[...], kc, (((1,), (1,)), ((), ())),
                            preferred_element_type=F32)
        m_prev = m_ref[...]
        m_new = jnp.maximum(m_prev, jnp.max(s, axis=-1, keepdims=True))
        alpha = jnp.exp2(m_prev - m_new)
        p = jnp.exp2(s - m_new)
        l_ref[...] = alpha * l_ref[...] + jnp.sum(p, axis=-1, keepdims=True)
        acc_ref[...] = alpha * acc_ref[...] + jnp.dot(p.astype(BF16), vc,
                                                      preferred_element_type=F32)
        m_ref[...] = m_new

    def new_body(c, carry):
        start = pl.multiple_of(c * ck_new, ck_new)
        step(k_ref[pl.ds(start, ck_new), :], v_ref[pl.ds(start, ck_new), :])
        return carry

    lax.fori_loop(0, n_new // ck_new, new_body, 0)
    for c in range(n_ctx // ck_ctx if n_ctx else 0):
        step(kx_ref[c * ck_ctx:(c + 1) * ck_ctx, :], vx_ref[c * ck_ctx:(c + 1) * ck_ctx, :])

    o = acc_ref[...] / l_ref[...]
    if mode == "diff":
        a = alam_ref[...]
        lam = (jnp.exp(jnp.sum(a[0:1, :] * a[1:2, :], axis=-1, keepdims=True))
               - jnp.exp(jnp.sum(a[2:3, :] * a[3:4, :], axis=-1, keepdims=True)) + lam_init)
        o_h0 = o[0:tq] - lam * o[tq:2 * tq]
        o_h1 = o[2 * tq:3 * tq] - lam * o[3 * tq:4 * tq]
        out = jnp.where(lane < HEAD_DIM, o_h0, o_h1)
        sq = out * out
        ms_lo = jnp.sum(jnp.where(lane < HEAD_DIM, sq, 0.0), axis=-1, keepdims=True)
        ms_hi = jnp.sum(jnp.where(lane < HEAD_DIM, 0.0, sq), axis=-1, keepdims=True)
        ms = jnp.where(lane < HEAD_DIM, ms_lo, ms_hi) * (1.0 / HEAD_DIM)
        out = out * lax.rsqrt(ms + RMS_EPS) * g_ref[0:1, :] * (1.0 - lam_init)
        o_ref[...] = out.astype(o_ref.dtype)
    else:
        for t in range(nt):
            lo = o[(2 * t) * tq:(2 * t + 1) * tq]
            hi = o[(2 * t + 1) * tq:(2 * t + 2) * tq]
            o_ref[:, t * LANE:(t + 1) * LANE] = jnp.where(lane < HEAD_DIM, lo, hi).astype(o_ref.dtype)


def _attn_full(proj, *, mode, tok_off, n_batch, seq, tq, q_tile, k_tile, v_tile,
               ctx_k=None, ctx_v=None, ctx_tile=0, sink=None, a_lambda=None, subln_g=None,
               lam_init=0.0, out_prev=None, out_width=None, name="attn"):
    t_all = proj.shape[0]
    nt = 1 if mode == "diff" else 3
    n_pair = 2 if mode == "diff" else 1
    n_grp = 4 if mode == "diff" else 2
    rows = nt * n_grp * tq
    nq = seq // tq
    qblk0 = tok_off // tq
    sblk0 = tok_off // seq
    ck_new = min(512, seq)
    n_ctx = 0 if ctx_k is None else ctx_k.shape[1]
    ck_ctx = min(512, n_ctx) if n_ctx else 0
    qw = nt * LANE
    qcol0 = q_tile // nt

    args, in_specs = [], []
    if sink is not None:
        args.append(sink)
        in_specs.append(pl.BlockSpec(memory_space=pltpu.SMEM))
    args += [proj, proj, proj]
    in_specs += [
        pl.BlockSpec((tq, qw), lambda b, j, i: (qblk0 + b * nq + i, qcol0 + j)),
        pl.BlockSpec((seq, LANE), lambda b, j, i: (sblk0 + b, k_tile + j)),
        pl.BlockSpec((seq, LANE), lambda b, j, i: (sblk0 + b, v_tile + j)),
    ]
    if n_ctx:
        args += [ctx_k, ctx_v]
        in_specs += [pl.BlockSpec((None, n_ctx, LANE), lambda b, j, i: (b, 0, ctx_tile + j)),
                     pl.BlockSpec((None, n_ctx, LANE), lambda b, j, i: (b, 0, ctx_tile + j))]
    if mode == "diff":
        args += [a_lambda, subln_g]
        in_specs += [pl.BlockSpec(a_lambda.shape, lambda b, j, i: (0, 0)),
                     pl.BlockSpec(subln_g.shape, lambda b, j, i: (0, 0))]
    aliases = {}
    if out_prev is not None:
        aliases = {len(args): 0}
        args.append(out_prev)
        in_specs.append(pl.BlockSpec(memory_space=pl.ANY))

    kern = functools.partial(_attn_full_kernel, mode=mode, nt=nt, tq=tq, ck_new=ck_new, n_new=seq,
                             ck_ctx=ck_ctx, n_ctx=n_ctx, has_sink=sink is not None, lam_init=lam_init)

    def wrapped(*refs):
        if out_prev is not None:
            n_in = len(args)
            refs = refs[:n_in - 1] + refs[n_in:]
        kern(*refs)

    return pl.pallas_call(
        wrapped,
        grid=(n_batch, n_pair, nq),
        in_specs=in_specs,
        out_specs=pl.BlockSpec((tq, qw), lambda b, j, i: (qblk0 + b * nq + i, j)),
        out_shape=jax.ShapeDtypeStruct((t_all, out_width), BF16),
        scratch_shapes=[pltpu.VMEM((rows, LANE), BF16), pltpu.VMEM((rows, 1), F32),
                        pltpu.VMEM((rows, 1), F32), pltpu.VMEM((rows, LANE), F32)],
        input_output_aliases=aliases,
        compiler_params=_cparams(("arbitrary", "arbitrary", "arbitrary")),
        name=name,
    )(*args)


def _attn_window_kernel(sink_ref, q_ref, kl_ref, km_ref, kr_ref, vl_ref, vm_ref, vr_ref,
                        kx_ref, vx_ref, prev_ref, o_ref, *, tq, nq):
    del prev_ref
    i = pl.program_id(1)
    n_ctx = kx_ref.shape[0]
    span = 3 * tq + n_ctx
    lane = _lane_iota((tq, LANE))

    qs = []
    for t in range(3):
        qt = q_ref[:, t * LANE:(t + 1) * LANE]
        for r in range(2):
            qs.append(jnp.where((lane // HEAD_DIM) == r, qt, jnp.zeros_like(qt)))
    qs = jnp.concatenate(qs, axis=0)
    kcat = jnp.concatenate([kl_ref[...], km_ref[...], kr_ref[...], kx_ref[...]], axis=0)
    vcat = jnp.concatenate([vl_ref[...], vm_ref[...], vr_ref[...], vx_ref[...]], axis=0)
    s = lax.dot_general(qs, kcat, (((1,), (1,)), ((), ())), preferred_element_type=F32)

    r_i = lax.broadcasted_iota(jnp.int32, (tq, span), 0)
    c_i = lax.broadcasted_iota(jnp.int32, (tq, span), 1)
    never = 1 << 20
    thr_l = jnp.where(i > 0, 0, never)
    thr_r = jnp.where(i < nq - 1, 0, never)
    ok_l = (c_i >= tq) | ((c_i - r_i) >= thr_l)
    ok_r = (c_i < 2 * tq) | (c_i >= 3 * tq) | ((r_i - (c_i - 2 * tq)) >= thr_r)
    keep = ok_l & ok_r

    outs = []
    for g in range(6):
        t, r = g // 2, g % 2
        sg = jnp.where(keep, s[g * tq:(g + 1) * tq], NEG)
        sk = sink_ref[t + 3 * r] * LOG2E
        m = jnp.maximum(jnp.max(sg, axis=-1, keepdims=True), sk)
        p = jnp.exp2(sg - m)
        l = jnp.sum(p, axis=-1, keepdims=True) + jnp.exp2(sk - m)
        o = jnp.dot(p.astype(BF16), vcat, preferred_element_type=F32) / l
        outs.append(o)
    for t in range(3):
        o_ref[:, t * LANE:(t + 1) * LANE] = jnp.where(lane < HEAD_DIM, outs[2 * t],
                                                      outs[2 * t + 1]).astype(o_ref.dtype)


def _attn_window(proj, ctx_k, ctx_v, sink, out_prev, *, tok_off, n_batch, seq):
    tq = WINDOW
    nq = seq // tq
    blk0 = tok_off // tq
    n_ctx = ctx_k.shape[1]

    def kv_spec(tile, delta):
        def imap(b, i):
            return (blk0 + b * nq + jnp.clip(i + delta, 0, nq - 1), tile)
        return pl.BlockSpec((tq, LANE), imap)

    return pl.pallas_call(
        functools.partial(_attn_window_kernel, tq=tq, nq=nq),
        grid=(n_batch, nq),
        in_specs=[pl.BlockSpec(memory_space=pltpu.SMEM),
                  pl.BlockSpec((tq, 3 * LANE), lambda b, i: (blk0 + b * nq + i, QC_T // 3)),
                  kv_spec(KC_T, -1), kv_spec(KC_T, 0), kv_spec(KC_T, 1),
                  kv_spec(VC_T, -1), kv_spec(VC_T, 0), kv_spec(VC_T, 1),
                  pl.BlockSpec((None, n_ctx, LANE), lambda b, i: (b, 0, 0)),
                  pl.BlockSpec((None, n_ctx, LANE), lambda b, i: (b, 0, 0)),
                  pl.BlockSpec(memory_space=pl.ANY)],
        out_specs=pl.BlockSpec((tq, 3 * LANE), lambda b, i: (blk0 + b * nq + i, 0)),
        out_shape=jax.ShapeDtypeStruct(out_prev.shape, BF16),
        input_output_aliases={10: 0},
        compiler_params=_cparams(("arbitrary", "arbitrary")),
        name="attn_c_window",
    )(sink, proj, proj, proj, proj, proj, proj, proj, ctx_k, ctx_v, out_prev)


def _layer_norm(y, g, b):
    mu = jnp.mean(y, axis=-1, keepdims=True)
    var = jnp.mean(jnp.square(y - mu), axis=-1, keepdims=True)
    return (y - mu) * lax.rsqrt(var + LN_EPS) * g + b


def _outproj_kernel(x_ref, oa_ref, ob_ref, oc_ref, wa_ref, wb_ref, wc_ref, mod_ref, ln_ref,
                    wr_ref, br_ref, x1_ref, h2_ref, gate_ref, idx_ref, *, alpha):
    tm = x_ref.shape[0]
    o = (jnp.dot(oa_ref[...], wa_ref[...], preferred_element_type=F32)
         + jnp.dot(ob_ref[...], wb_ref[...], preferred_element_type=F32)
         + jnp.dot(oc_ref[...], wc_ref[...], preferred_element_type=F32))
    gate1 = mod_ref[2:3, :]
    shift2 = mod_ref[3:4, :]
    scale2 = mod_ref[4:5, :]
    x1 = _layer_norm(alpha * x_ref[...] + gate1 * o, ln_ref[0:1, :], ln_ref[1:2, :])
    x1_ref[...] = x1
    h2 = x1 * (1.0 + scale2) + shift2
    h2_ref[...] = h2

    h_hi = h2.astype(BF16)
    h_lo = (h2 - h_hi.astype(F32)).astype(BF16)
    logits = (jnp.dot(h_hi, wr_ref[0], preferred_element_type=F32)
              + jnp.dot(h_hi, wr_ref[1], preferred_element_type=F32)
              + jnp.dot(h_lo, wr_ref[0], preferred_element_type=F32)
              + br_ref[0:1, :])

    lane = _lane_iota((tm, LANE))
    lane_f = lane.astype(F32)
    cur = logits
    vals, idxs = [], []
    for _ in range(TOP_K):
        mx = jnp.max(cur, axis=-1, keepdims=True)
        ix = jnp.min(jnp.where(cur == mx, lane_f, float(LANE)), axis=-1, keepdims=True)
        vals.append(mx)
        idxs.append(ix)
        cur = jnp.where(lane_f == ix, -jnp.inf, cur)
    es = [jnp.exp(v - vals[0]) for v in vals]
    den = es[0] + es[1] + es[2] + es[3]
    gates = jnp.zeros((tm, LANE), F32)
    idx = jnp.zeros((tm, LANE), F32)
    for k in range(TOP_K):
        gates = jnp.where(lane == k, es[k] / den, gates)
        idx = jnp.where(lane == k, idxs[k], idx)
    gate_ref[...] = gates
    idx_ref[...] = idx.astype(jnp.int32)


def _outproj(x, oa, ob, oc, wa, wb, wc, mod_l, ln, wr, br, *, tm, t_ctx, s_dec, alpha):
    t_all, d = x.shape
    n_ctx_tiles = t_ctx // tm
    tiles_per_seq = s_dec // tm

    def grp(i):
        return jnp.where(i < n_ctx_tiles, 0, 1 + (i - n_ctx_tiles) // tiles_per_seq)

    row = lambda w: pl.BlockSpec((tm, w), lambda i: (i, 0))
    full = lambda a: pl.BlockSpec(a.shape, lambda i: (0,) * a.ndim)
    return pl.pallas_call(
        functools.partial(_outproj_kernel, alpha=alpha),
        grid=(t_all // tm,),
        in_specs=[row(d), row(oa.shape[1]), row(ob.shape[1]), row(oc.shape[1]),
                  full(wa), full(wb), full(wc),
                  pl.BlockSpec((None, 6, d), lambda i: (grp(i), 0, 0)),
                  full(ln), full(wr), full(br)],
        out_specs=[row(d), row(d), row(LANE), row(LANE)],
        out_shape=[jax.ShapeDtypeStruct((t_all, d), F32), jax.ShapeDtypeStruct((t_all, d), F32),
                   jax.ShapeDtypeStruct((t_all, LANE), F32),
                   jax.ShapeDtypeStruct((t_all, LANE), jnp.int32)],
        compiler_params=_cparams(("arbitrary",)),
        name="outproj_ln_router",
    )(x, oa, ob, oc, wa, wb, wc, mod_l, ln, wr, br)


def _row_gather_copy(src_hbm, row, dst, dst_row, sem):
    return pltpu.make_async_copy(src_hbm.at[pl.ds(row, 1)], dst.at[pl.ds(dst_row, 1)], sem)


def _moe_kernel(be_ref, nused_ref, src_cur, src_next, h_hbm, wgu_ref, bgu_ref, wdn_ref, bdn_ref,
                o_ref, xbuf, sems):
    del be_ref
    i = pl.program_id(0)
    n_used = nused_ref[0]
    blk = xbuf.shape[1]

    def issue(src_ref, slot):
        def body(r, carry):
            _row_gather_copy(h_hbm, src_ref[0, r], xbuf.at[slot], r, sems.at[slot]).start()
            return carry
        lax.fori_loop(0, blk, body, 0, unroll=8)

    @pl.when(i == 0)
    def _():
        issue(src_cur, 0)

    @pl.when(i < n_used)
    def _():
        slot = i % 2
        pltpu.make_async_copy(h_hbm.at[pl.ds(0, blk)], xbuf.at[slot], sems.at[slot]).wait()

        @pl.when(i + 1 < n_used)
        def _():
            issue(src_next, 1 - slot)

        x = xbuf[slot].astype(BF16)
        gu = jnp.dot(x, wgu_ref[...], preferred_element_type=F32) + bgu_ref[...]
        d_ff = gu.shape[1] // 2
        glu = jnp.minimum(gu[:, :d_ff], SWIGLU_LIMIT)
        lin = jnp.clip(gu[:, d_ff:], -SWIGLU_LIMIT, SWIGLU_LIMIT)
        act = glu * jax.nn.sigmoid(SWIGLU_ALPHA * glu) * (lin + 1.0)
        o_ref[...] = jnp.dot(act.astype(BF16), wdn_ref[...], preferred_element_type=F32) + bdn_ref[...]

    @pl.when(i >= n_used)
    def _():
        o_ref[...] = jnp.zeros(o_ref.shape, F32)


def _moe(h2, block_e, row_src, n_used, wgu, bgu, wdn, bdn):
    n_blocks = block_e.shape[0]
    d = h2.shape[1]
    n_e, _, n_gu = wgu.shape
    src3 = row_src.reshape(n_blocks, 1, MOE_BLOCK)
    grid_spec = pltpu.PrefetchScalarGridSpec(
        num_scalar_prefetch=2,
        grid=(n_blocks,),
        in_specs=[pl.BlockSpec((None, 1, MOE_BLOCK), lambda i, be, nu: (i, 0, 0),
                               memory_space=pltpu.SMEM),
                  pl.BlockSpec((None, 1, MOE_BLOCK), lambda i, be, nu: (jnp.minimum(i + 1, n_blocks - 1), 0, 0),
                               memory_space=pltpu.SMEM),
                  pl.BlockSpec(memory_space=pl.ANY),
                  pl.BlockSpec((None, d, n_gu), lambda i, be, nu: (be[i], 0, 0)),
                  pl.BlockSpec((None, 1, n_gu), lambda i, be, nu: (be[i], 0, 0)),
                  pl.BlockSpec((None, n_gu // 2, d), lambda i, be, nu: (be[i], 0, 0)),
                  pl.BlockSpec((None, 1, d), lambda i, be, nu: (be[i], 0, 0))],
        out_specs=pl.BlockSpec((MOE_BLOCK, d), lambda i, be, nu: (i, 0)),
        scratch_shapes=[pltpu.VMEM((2, MOE_BLOCK, d), F32), pltpu.SemaphoreType.DMA((2,))],
    )
    return pl.pallas_call(
        _moe_kernel,
        grid_spec=grid_spec,
        out_shape=jax.ShapeDtypeStruct((n_blocks * MOE_BLOCK, d), F32),
        compiler_params=_cparams(("arbitrary",)),
        name="moe_experts",
    )(block_e, n_used, src3, src3, h2, wgu, bgu.reshape(n_e, 1, n_gu), wdn, bdn.reshape(n_e, 1, d))


def _combine_kernel(dst_cur, dst_next, x1_ref, gate_ref, mod_ref, ln_ref, y_hbm, o_ref, gbuf, sems,
                    *, alpha):
    i = pl.program_id(0)
    n = pl.num_programs(0)
    tm = x1_ref.shape[0]

    def issue(dst_ref, slot):
        def body(t, carry):
            for k in range(TOP_K):
                _row_gather_copy(y_hbm, dst_ref[0, t * TOP_K + k], gbuf.at[slot, k], t,
                                 sems.at[slot]).start()
            return carry
        lax.fori_loop(0, tm, body, 0, unroll=4)

    @pl.when(i == 0)
    def _():
        issue(dst_cur, 0)

    slot = i % 2
    for k in range(TOP_K):
        pltpu.make_async_copy(y_hbm.at[pl.ds(0, tm)], gbuf.at[slot, k], sems.at[slot]).wait()

    @pl.when(i + 1 < n)
    def _():
        issue(dst_next, 1 - slot)

    gates = gate_ref[...]
    y = gates[:, 0:1] * gbuf[slot, 0]
    for k in range(1, TOP_K):
        y = y + gates[:, k:k + 1] * gbuf[slot, k]
    gate2 = mod_ref[5:6, :]
    o_ref[...] = _layer_norm(alpha * x1_ref[...] + gate2 * y, ln_ref[0:1, :], ln_ref[1:2, :])


def _combine(x1, gates, dest, y_sorted, mod_l, ln, *, tm, t_ctx, s_dec, alpha):
    t_all, d = x1.shape
    n_ctx_tiles = t_ctx // tm
    tiles_per_seq = s_dec // tm

    def grp(i):
        return jnp.where(i < n_ctx_tiles, 0, 1 + (i - n_ctx_tiles) // tiles_per_seq)

    n_tiles = t_all // tm
    dst3 = dest.reshape(n_tiles, 1, tm * TOP_K)
    return pl.pallas_call(
        functools.partial(_combine_kernel, alpha=alpha),
        grid=(n_tiles,),
        in_specs=[pl.BlockSpec((None, 1, tm * TOP_K), lambda i: (i, 0, 0), memory_space=pltpu.SMEM),
                  pl.BlockSpec((None, 1, tm * TOP_K), lambda i: (jnp.minimum(i + 1, n_tiles - 1), 0, 0),
                               memory_space=pltpu.SMEM),
                  pl.BlockSpec((tm, d), lambda i: (i, 0)),
                  pl.BlockSpec((tm, LANE), lambda i: (i, 0)),
                  pl.BlockSpec((None, 6, d), lambda i: (grp(i), 0, 0)),
                  pl.BlockSpec(ln.shape, lambda i: (0, 0)),
                  pl.BlockSpec(memory_space=pl.ANY)],
        out_specs=pl.BlockSpec((tm, d), lambda i: (i, 0)),
        out_shape=jax.ShapeDtypeStruct((t_all, d), F32),
        scratch_shapes=[pltpu.VMEM((2, TOP_K, tm, d), F32), pltpu.SemaphoreType.DMA((2,))],
        compiler_params=_cparams(("arbitrary",)),
        name="combine_ln",
    )(dst3, dst3, x1, gates, mod_l, ln, y_sorted)


def _routing(idx):
    n_tok = idx.shape[0]
    n_asg = n_tok * TOP_K
    flat_e = idx.reshape(-1)
    order = jnp.argsort(flat_e).astype(jnp.int32)
    counts = jnp.sum((flat_e[:, None] == jnp.arange(N_EXPERTS)[None, :]).astype(jnp.int32), axis=0)
    starts = jnp.cumsum(counts) - counts
    padded = (counts + MOE_BLOCK - 1) // MOE_BLOCK * MOE_BLOCK
    pends = jnp.cumsum(padded)
    pstarts = pends - padded
    n_blocks = -(-n_asg // MOE_BLOCK) + N_EXPERTS
    n_rows = n_blocks * MOE_BLOCK
    sorted_e = flat_e[order]
    dest_sorted = (pstarts[sorted_e] + jnp.arange(n_asg, dtype=jnp.int32) - starts[sorted_e]).astype(jnp.int32)
    dest = jnp.zeros((n_asg,), jnp.int32).at[order].set(dest_sorted)
    rows = jnp.arange(n_rows, dtype=jnp.int32)
    row_e = jnp.minimum(jnp.searchsorted(pends, rows, side='right'), N_EXPERTS - 1).astype(jnp.int32)
    rank = rows - pstarts[row_e]
    valid = rank < counts[row_e]
    src_pos = jnp.clip(starts[row_e] + rank, 0, n_asg - 1)
    row_src = jnp.where(valid, order[src_pos] // TOP_K, 0).astype(jnp.int32)
    block_e = row_e[::MOE_BLOCK]
    n_used = (pends[-1] // MOE_BLOCK).astype(jnp.int32).reshape(1)
    return block_e, row_src, n_used, dest


def _in_perm():
    qa, ka, va = 0, 256, 512
    qb, kb, vb = 768, 1152, 1280
    qc, kc, vc = 1408, 1792, 1920
    cols = []
    for base in (qb, qc):
        for h in GQA_ORDER:
            cols += list(range(base + h * HEAD_DIM, base + (h + 1) * HEAD_DIM))
    for base, width in ((qa, 256), (ka, 256), (kb, 128), (kc, 128), (va, 256), (vb, 128), (vc, 128)):
        cols += list(range(base, base + width))
    return jnp.asarray(cols, jnp.int32)


def _gqa_rows(base):
    rows = []
    for h in GQA_ORDER:
        rows += list(range(base + h * HEAD_DIM, base + (h + 1) * HEAD_DIM))
    return jnp.asarray(rows, jnp.int32)


def _rope_patterns(rows, tm):
    def tables(dim):
        row = jnp.repeat(jnp.arange(rows), GRID_W).astype(F32)
        col = jnp.tile(jnp.arange(GRID_W), rows).astype(F32)
        nf = dim // 4
        freqs = ROPE_THETA ** (-jnp.arange(nf, dtype=F32) / nf)
        ang = jnp.concatenate([row[:, None] * freqs, col[:, None] * freqs], -1)
        return jnp.cos(ang), jnp.sin(ang)

    out = []
    for dim in (A_QK_DIM, HEAD_DIM):
        cos, sin = tables(dim)
        nf = dim // 4
        cr, cc, sr, sc = cos[:, :nf], cos[:, nf:], sin[:, :nf], sin[:, nf:]
        cpat = jnp.concatenate([cr, cr, cc, cc], -1)
        spat = jnp.concatenate([-sr, sr, -sc, sc], -1)
        reps = LANE // dim
        out += [jnp.tile(cpat, (1, reps)), jnp.tile(spat, (1, reps))]
    tab = jnp.stack(out, 0)
    ident = jnp.stack([jnp.ones((tm, LANE), F32), jnp.zeros((tm, LANE), F32)] * 2, 0)
    return jnp.concatenate([tab, ident], axis=1)


def _ctx_cache(cache):
    b, l, p, h, d = cache.shape
    return jnp.transpose(cache, (1, 0, 2, 3, 4)).reshape(l, b, p, h * d).astype(BF16)


def kernel(x_prompt, x_sample, cache_a_k, cache_a_v, cache_b_k, cache_b_v, cache_c_k, cache_c_v, c, c_ctx, w_mod, b_mod, w_in, a_lambda, a_subln_g, b_q_norm_g, b_k_norm_g, c_sink, w_out, ln1_g, ln1_b, w_router, b_router, w_gate_up, b_gate_up, w_down, b_down, ln2_g, ln2_b):
    depth = w_in.shape[0]
    n_ctx_b, s_ctx, d = x_prompt.shape
    n_dec_b, s_dec, _ = x_sample.shape
    t_ctx = n_ctx_b * s_ctx
    t_dec = n_dec_b * s_dec
    t_all = t_ctx + t_dec
    alpha = (2 * depth) ** 0.25
    tm = 512 if (t_ctx % 512 == 0 and s_dec % 512 == 0) else 256
    tm_c = 256

    x = jnp.concatenate([x_prompt.reshape(t_ctx, d), x_sample.reshape(t_dec, d)], axis=0)

    n_grp = 1 + n_dec_b
    g_pad = -(-n_grp // 8) * 8
    cond = jnp.zeros((g_pad, d), F32).at[0].set(c_ctx).at[1:n_grp].set(c)
    mod = _modulation(cond, w_mod, b_mod).reshape(depth, g_pad, 6, d)

    perm = _in_perm()
    w_in_p = jnp.take(w_in, perm, axis=2).astype(BF16)
    wa = w_out[:, 0:256].astype(BF16)
    wb = jnp.take(w_out, _gqa_rows(256), axis=1).astype(BF16)
    wc = jnp.take(w_out, _gqa_rows(640), axis=1).astype(BF16)
    wgu = w_gate_up.astype(BF16)
    wdn = w_down.astype(BF16)

    pad_e = LANE - N_EXPERTS
    wr_f = jnp.pad(w_router, ((0, 0), (0, 0), (0, pad_e)))
    wr_hi = wr_f.astype(BF16)
    wr_lo = (wr_f - wr_hi.astype(F32)).astype(BF16)
    wr = jnp.stack([wr_hi, wr_lo], axis=1)
    br = jnp.pad(b_router, ((0, 0), (0, pad_e)), constant_values=NEG).reshape(depth, 1, LANE)
    br = jnp.broadcast_to(br, (depth, 8, LANE))

    rope_tab = _rope_patterns(s_dec // GRID_W, tm)
    tile2 = lambda g: jnp.tile(g, (1, LANE // HEAD_DIM))
    gains = jnp.zeros((depth, 8, LANE), F32).at[:, 0].set(tile2(b_q_norm_g)).at[:, 1].set(tile2(b_k_norm_g))
    subln = jnp.broadcast_to(tile2(a_subln_g)[:, None, :], (depth, 8, LANE))
    ln1 = jnp.stack([ln1_g, ln1_b], axis=1)
    ln2 = jnp.stack([ln2_g, ln2_b], axis=1)

    xa_k, xa_v = _ctx_cache(cache_a_k), _ctx_cache(cache_a_v)
    xb_k, xb_v = _ctx_cache(cache_b_k), _ctx_cache(cache_b_v)
    xc_k, xc_v = _ctx_cache(cache_c_k), _ctx_cache(cache_c_v)

    tq_ctx = min(256, s_ctx)
    tq_dec = 256
    kv_layers = []
    for l in range(depth):
        lam0 = _lambda_init(l)
        mod_l = mod[l]
        proj, kv = _inproj(x, mod_l, w_in_p[l], rope_tab, gains[l], tm=tm, t_ctx=t_ctx, s_dec=s_dec)
        kv_layers.append(kv)

        common_a = dict(mode="diff", q_tile=QA_T, k_tile=KA_T, v_tile=VA_T, a_lambda=a_lambda[l],
                        subln_g=subln[l], lam_init=lam0, out_width=A_HEADS * HEAD_DIM)
        oa = _attn_full(proj, tok_off=0, n_batch=n_ctx_b, seq=s_ctx, tq=tq_ctx, name="attn_a_ctx", **common_a)
        oa = _attn_full(proj, tok_off=t_ctx, n_batch=n_dec_b, seq=s_dec, tq=tq_dec, ctx_k=xa_k[l],
                        ctx_v=xa_v[l], out_prev=oa, name="attn_a_dec", **common_a)

        common_b = dict(mode="gqa", q_tile=QB_T, k_tile=KB_T, v_tile=VB_T, out_width=B_HEADS * HEAD_DIM)
        ob = _attn_full(proj, tok_off=0, n_batch=n_ctx_b, seq=s_ctx, tq=tq_ctx // 2, name="attn_b_ctx", **common_b)
        ob = _attn_full(proj, tok_off=t_ctx, n_batch=n_dec_b, seq=s_dec, tq=tq_dec // 2, ctx_k=xb_k[l],
                        ctx_v=xb_v[l], out_prev=ob, name="attn_b_dec", **common_b)

        sink_p = c_sink[l]
        oc = _attn_full(proj, mode="gqa", tok_off=0, n_batch=n_ctx_b, seq=s_ctx, tq=tq_ctx // 2,
                        q_tile=QC_T, k_tile=KC_T, v_tile=VC_T, sink=sink_p,
                        out_width=C_HEADS * HEAD_DIM, name="attn_c_ctx")
        oc = _attn_window(proj, xc_k[l], xc_v[l], sink_p, oc, tok_off=t_ctx, n_batch=n_dec_b, seq=s_dec)

        x1, h2, gates, idx = _outproj(x, oa, ob, oc, wa[l], wb[l], wc[l], mod_l, ln1[l], wr[l], br[l],
                                      tm=tm, t_ctx=t_ctx, s_dec=s_dec, alpha=alpha)
        block_e, row_src, n_used, dest = _routing(idx[:, :TOP_K])
        y_sorted = _moe(h2, block_e, row_src, n_used, wgu[l], b_gate_up[l], wdn[l], b_down[l])
        x = _combine(x1, gates, dest, y_sorted, mod_l, ln2[l], tm=tm_c, t_ctx=t_ctx, s_dec=s_dec, alpha=alpha)

    y = x[:t_ctx].reshape(n_ctx_b, s_ctx, d)
    z = x[t_ctx:].reshape(n_dec_b, s_dec, d)
    kv_all = jnp.stack(kv_layers, axis=1).reshape(n_ctx_b, s_ctx, depth, 8 * LANE)
    kv_all = jnp.transpose(kv_all, (0, 2, 1, 3))

    def cache_out(lo, width, heads):
        return kv_all[..., lo:lo + width].reshape(n_ctx_b, depth, s_ctx, heads, width // heads)

    new_a_k = cache_out(0, 256, A_HEADS)
    new_b_k = cache_out(256, 128, B_KV_HEADS)
    new_c_k = cache_out(384, 128, C_KV_HEADS)
    new_a_v = cache_out(512, 256, A_HEADS)
    new_b_v = cache_out(768, 128, B_KV_HEADS)
    new_c_v = cache_out(896, 128, C_KV_HEADS)
    return (y, z, new_a_k, new_a_v, new_b_k, new_b_v, new_c_k, new_c_v)
```

```python
import functools
import math

import jax
import jax.numpy as jnp
from jax import lax
from jax.experimental import pallas as pl
from jax.experimental.pallas import tpu as pltpu

F32 = jnp.float32
BF16 = jnp.bfloat16

LANE = 128
HEAD_DIM = 64
A_HEADS = 4
A_QK_DIM = 32
B_HEADS = 6
B_KV_HEADS = 2
C_HEADS = 6
C_KV_HEADS = 2
GRID_W = 64
WINDOW = 128
ROPE_THETA = 10000.0
N_EXPERTS = 32
TOP_K = 4
SWIGLU_ALPHA = 1.702
SWIGLU_LIMIT = 7.0
MOE_BLOCK = 256
LN_EPS = 1e-5
RMS_EPS = 1e-6
NEG = -1e30
LOG2E = math.log2(math.e)
VMEM_LIMIT = 56 * 1024 * 1024

QB_T, QC_T, QA_T, KA_T, KB_T, KC_T, VA_T, VB_T, VC_T = 0, 3, 6, 8, 10, 11, 12, 14, 15
GQA_ORDER = (0, 3, 1, 4, 2, 5)


def _lambda_init(layer):
    return 0.8 - 0.6 * math.exp(-0.3 * layer)


def _lane_iota(shape):
    return lax.broadcasted_iota(jnp.int32, shape, len(shape) - 1)


def _cparams(sem, vmem=VMEM_LIMIT):
    return pltpu.CompilerParams(dimension_semantics=sem, vmem_limit_bytes=vmem)


def _mod_kernel(c_ref, w_ref, b_ref, o_ref):
    c = c_ref[...]
    s = (c * jax.nn.sigmoid(c)).astype(BF16)
    o_ref[...] = jnp.dot(s, w_ref[...].astype(BF16), preferred_element_type=F32) + b_ref[...]


def _modulation(cond, w_mod, b_mod):
    depth, d, n = w_mod.shape
    g = cond.shape[0]
    tn = 1536
    return pl.pallas_call(
        _mod_kernel,
        grid=(depth, n // tn),
        in_specs=[pl.BlockSpec((g, d), lambda l, j: (0, 0)),
                  pl.BlockSpec((None, d, tn), lambda l, j: (l, 0, j)),
                  pl.BlockSpec((None, 1, tn), lambda l, j: (l, 0, j))],
        out_specs=pl.BlockSpec((None, g, tn), lambda l, j: (l, 0, j)),
        out_shape=jax.ShapeDtypeStruct((depth, g, n), F32),
        compiler_params=_cparams(("arbitrary", "arbitrary")),
        name="modulation",
    )(cond, w_mod, b_mod.reshape(depth, 1, n))


def _swap_blocks(x, blk, lane):
    up = pltpu.roll(x, LANE - blk, 1)
    dn = pltpu.roll(x, blk, 1)
    return jnp.where((lane % (2 * blk)) < blk, up, dn)


def _segment_mean_sq(x, ones_seg):
    sq = x * x
    hi = sq.astype(BF16)
    lo = (sq - hi.astype(F32)).astype(BF16)
    tot = (jnp.dot(hi, ones_seg, preferred_element_type=F32)
           + jnp.dot(lo, ones_seg, preferred_element_type=F32))
    return tot * (1.0 / HEAD_DIM)


def _inproj_kernel(x_ref, mod_ref, w_ref, rope_ref, g_ref, proj_ref, kv_ref, *, n_ctx_tiles):
    i = pl.program_id(0)
    tm = x_ref.shape[0]
    x = x_ref[...]
    shift1 = mod_ref[0:1, :]
    scale1 = mod_ref[1:2, :]
    h = (x * (1.0 + scale1) + shift1).astype(BF16)

    lane = _lane_iota((tm, LANE))
    r_i = lax.broadcasted_iota(jnp.int32, (LANE, LANE), 0)
    c_i = lax.broadcasted_iota(jnp.int32, (LANE, LANE), 1)
    ones_seg = jnp.where((r_i // HEAD_DIM) == (c_i // HEAD_DIM), 1.0, 0.0).astype(BF16)

    cos_a, sin_a, cos_b, sin_b = rope_ref[0], rope_ref[1], rope_ref[2], rope_ref[3]
    gq = g_ref[0:1, :]
    gk = g_ref[1:2, :]
    qa_scale = (A_QK_DIM ** -0.5) * LOG2E
    q_scale = (HEAD_DIM ** -0.5) * LOG2E

    def section(t0, nt):
        return jnp.dot(h, w_ref[:, t0 * LANE:(t0 + nt) * LANE], preferred_element_type=F32)

    def tile(sec, t):
        return sec[:, t * LANE:(t + 1) * LANE]

    def rope(xt, cos, sin, blk):
        return xt * cos + _swap_blocks(xt, blk, lane) * sin

    def rms(xt, g):
        return xt * lax.rsqrt(_segment_mean_sq(xt, ones_seg) + RMS_EPS) * g

    def put(t, val):
        proj_ref[:, t * LANE:(t + 1) * LANE] = val.astype(BF16)

    kv_tiles = []

    sec = section(QB_T, 3)
    for t in range(3):
        put(QB_T + t, rope(rms(tile(sec, t), gq), cos_b, sin_b, 16) * q_scale)
    sec = section(QC_T, 3)
    for t in range(3):
        put(QC_T + t, rope(tile(sec, t), cos_b, sin_b, 16) * q_scale)
    sec = section(QA_T, 4)
    for t in range(2):
        put(QA_T + t, rope(tile(sec, t), cos_a, sin_a, 8) * qa_scale)
    for t in range(2):
        ka = rope(tile(sec, 2 + t), cos_a, sin_a, 8)
        put(KA_T + t, ka)
        kv_tiles.append(ka)
    sec = section(KB_T, 2)
    kb = rope(rms(tile(sec, 0), gk), cos_b, sin_b, 16)
    put(KB_T, kb)
    kc = rope(tile(sec, 1), cos_b, sin_b, 16)
    put(KC_T, kc)
    kv_tiles += [kb, kc]
    sec = section(VA_T, 4)
    for t in range(4):
        put(VA_T + t, tile(sec, t))
        kv_tiles.append(tile(sec, t))

    @pl.when(i < n_ctx_tiles)
    def _():
        for t, val in enumerate(kv_tiles):
            kv_ref[:, t * LANE:(t + 1) * LANE] = val


def _inproj(x, mod_l, w_in_p, rope_tab, gains, *, tm, t_ctx, s_dec):
    t_all, d = x.shape
    n_ctx_tiles = t_ctx // tm
    tiles_per_seq = s_dec // tm
    s_rope_tiles = (rope_tab.shape[1] - tm) // tm

    def grp(i):
        return jnp.where(i < n_ctx_tiles, 0, 1 + (i - n_ctx_tiles) // tiles_per_seq)

    def rope_blk(i):
        return jnp.where(i < n_ctx_tiles, s_rope_tiles, (i - n_ctx_tiles) % tiles_per_seq)

    n_out = w_in_p.shape[1]
    return pl.pallas_call(
        functools.partial(_inproj_kernel, n_ctx_tiles=n_ctx_tiles),
        grid=(t_all // tm,),
        in_specs=[pl.BlockSpec((tm, d), lambda i: (i, 0)),
                  pl.BlockSpec((None, 6, d), lambda i: (grp(i), 0, 0)),
                  pl.BlockSpec((d, n_out), lambda i: (0, 0)),
                  pl.BlockSpec((4, tm, LANE), lambda i: (0, rope_blk(i), 0)),
                  pl.BlockSpec((8, LANE), lambda i: (0, 0))],
        out_specs=[pl.BlockSpec((tm, n_out), lambda i: (i, 0)),
                   pl.BlockSpec((tm, 8 * LANE), lambda i: (jnp.minimum(i, n_ctx_tiles - 1), 0))],
        out_shape=[jax.ShapeDtypeStruct((t_all, n_out), BF16),
                   jax.ShapeDtypeStruct((t_ctx, 8 * LANE), F32)],
        compiler_params=_cparams(("arbitrary",)),
        name="inproj",
    )(x, mod_l, w_in_p, rope_tab, gains)


def _attn_full_kernel(*refs, mode, nt, tq, n_new, n_ctx, has_sink, lam_init):
    refs = list(refs)
    sink_ref = refs.pop(0) if has_sink else None
    q_ref, k_ref, v_ref = refs[:3]
    refs = refs[3:]
    if n_ctx:
        kx_ref, vx_ref = refs[:2]
        refs = refs[2:]
    if mode == "diff":
        alam_ref, g_ref = refs[:2]
        refs = refs[2:]
    o_ref, qs_ref, s_ref, p_ref, l_ref = refs

    n_grp = 4 if mode == "diff" else 2
    gw = LANE // n_grp
    rows = nt * n_grp * tq
    lane = _lane_iota((tq, LANE))

    for t in range(nt):
        qt = q_ref[:, t * LANE:(t + 1) * LANE]
        for r in range(n_grp):
            row0 = (t * n_grp + r) * tq
            qs_ref[row0:row0 + tq, :] = jnp.where((lane // gw) == r, qt, jnp.zeros_like(qt))

    kt = min(512, n_new)
    tiles = [(k_ref, c * kt) for c in range(n_new // kt)]
    if n_ctx:
        tiles += [(kx_ref, c * kt) for c in range(n_ctx // kt)]
    rb = min(128, tq)
    rc = min(256, rows)

    accs = []
    for c0 in range(0, rows, rc):
        for c, (ref, off) in enumerate(tiles):
            s_ref[c0:c0 + rc, c * kt:(c + 1) * kt] = lax.dot_general(
                qs_ref[c0:c0 + rc, :], ref[off:off + kt, :], (((1,), (1,)), ((), ())),
                preferred_element_type=F32)
        for r0 in range(c0, c0 + rc, rb):
            m = jnp.max(s_ref[r0:r0 + rb, :], axis=-1, keepdims=True)
            if has_sink:
                g = r0 // tq
                sk = sink_ref[(g // n_grp) + 3 * (g % n_grp)] * LOG2E
                m = jnp.maximum(m, sk)
            p = jnp.exp2(s_ref[r0:r0 + rb, :] - m)
            l = jnp.sum(p, axis=-1, keepdims=True)
            if has_sink:
                l = l + jnp.exp2(sk - m)
            l_ref[r0:r0 + rb, :] = l
            p_ref[r0:r0 + rb, :] = p.astype(BF16)
        acc = jnp.dot(p_ref[c0:c0 + rc, :n_new], v_ref[...], preferred_element_type=F32)
        if n_ctx:
            acc = acc + jnp.dot(p_ref[c0:c0 + rc, n_new:], vx_ref[...], preferred_element_type=F32)
        accs.append(acc)
    o = jnp.concatenate(accs, axis=0) / l_ref[...]
    if mode == "diff":
        a = alam_ref[...]
        lam = (jnp.exp(jnp.sum(a[0:1, :] * a[1:2, :], axis=-1, keepdims=True))
               - jnp.exp(jnp.sum(a[2:3, :] * a[3:4, :], axis=-1, keepdims=True)) + lam_init)
        o_h0 = o[0:tq] - lam * o[tq:2 * tq]
        o_h1 = o[2 * tq:3 * tq] - lam * o[3 * tq:4 * tq]
        out = jnp.where(lane < HEAD_DIM, o_h0, o_h1)
        sq = out * out
        ms_lo = jnp.sum(jnp.where(lane < HEAD_DIM, sq, 0.0), axis=-1, keepdims=True)
        ms_hi = jnp.sum(jnp.where(lane < HEAD_DIM, 0.0, sq), axis=-1, keepdims=True)
        ms = jnp.where(lane < HEAD_DIM, ms_lo, ms_hi) * (1.0 / HEAD_DIM)
        out = out * lax.rsqrt(ms + RMS_EPS) * g_ref[0:1, :] * (1.0 - lam_init)
        o_ref[...] = out.astype(o_ref.dtype)
    else:
        for t in range(nt):
            lo = o[(2 * t) * tq:(2 * t + 1) * tq]
            hi = o[(2 * t + 1) * tq:(2 * t + 2) * tq]
            o_ref[:, t * LANE:(t + 1) * LANE] = jnp.where(lane < HEAD_DIM, lo, hi).astype(o_ref.dtype)


def _attn_full(proj, *, mode, tok_off, n_batch, seq, tq, q_tile, k_tile, v_tile,
               ctx_k=None, ctx_v=None, ctx_tile=0, sink=None, a_lambda=None, subln_g=None,
               lam_init=0.0, out_prev=None, out_width=None, name="attn"):
    t_all = proj.shape[0]
    nt = 1 if mode == "diff" else 3
    n_pair = 2 if mode == "diff" else 1
    n_grp = 4 if mode == "diff" else 2
    rows = nt * n_grp * tq
    nq = seq // tq
    qblk0 = tok_off // tq
    sblk0 = tok_off // seq
    n_ctx = 0 if ctx_k is None else ctx_k.shape[1]
    n_keys = seq + n_ctx
    qw = nt * LANE
    qcol0 = q_tile // nt

    args, in_specs = [], []
    if sink is not None:
        args.append(sink)
        in_specs.append(pl.BlockSpec(memory_space=pltpu.SMEM))
    args += [proj, proj, proj]
    in_specs += [
        pl.BlockSpec((tq, qw), lambda b, j, i: (qblk0 + b * nq + i, qcol0 + j)),
        pl.BlockSpec((seq, LANE), lambda b, j, i: (sblk0 + b, k_tile + j)),
        pl.BlockSpec((seq, LANE), lambda b, j, i: (sblk0 + b, v_tile + j)),
    ]
    if n_ctx:
        args += [ctx_k, ctx_v]
        in_specs += [pl.BlockSpec((None, n_ctx, LANE), lambda b, j, i: (b, 0, ctx_tile + j)),
                     pl.BlockSpec((None, n_ctx, LANE), lambda b, j, i: (b, 0, ctx_tile + j))]
    if mode == "diff":
        args += [a_lambda, subln_g]
        in_specs += [pl.BlockSpec(a_lambda.shape, lambda b, j, i: (0, 0)),
                     pl.BlockSpec(subln_g.shape, lambda b, j, i: (0, 0))]
    aliases = {}
    if out_prev is not None:
        aliases = {len(args): 0}
        args.append(out_prev)
        in_specs.append(pl.BlockSpec(memory_space=pl.ANY))

    kern = functools.partial(_attn_full_kernel, mode=mode, nt=nt, tq=tq, n_new=seq,
                             n_ctx=n_ctx, has_sink=sink is not None, lam_init=lam_init)

    def wrapped(*refs):
        if out_prev is not None:
            n_in = len(args)
            refs = refs[:n_in - 1] + refs[n_in:]
        kern(*refs)

    return pl.pallas_call(
        wrapped,
        grid=(n_batch, n_pair, nq),
        in_specs=in_specs,
        out_specs=pl.BlockSpec((tq, qw), lambda b, j, i: (qblk0 + b * nq + i, j)),
        out_shape=jax.ShapeDtypeStruct((t_all, out_width), BF16),
        scratch_shapes=[pltpu.VMEM((rows, LANE), BF16), pltpu.VMEM((rows, n_keys), F32),
                        pltpu.VMEM((rows, n_keys), BF16), pltpu.VMEM((rows, 1), F32)],
        input_output_aliases=aliases,
        compiler_params=_cparams(("arbitrary", "arbitrary", "arbitrary")),
        name=name,
    )(*args)


def _attn_window_kernel(sink_ref, q_ref, kl_ref, km_ref, kr_ref, vl_ref, vm_ref, vr_ref,
                        kx_ref, vx_ref, prev_ref, o_ref, *, tq, nq):
    del prev_ref
    i = pl.program_id(1)
    n_ctx = kx_ref.shape[0]
    span = 3 * tq + n_ctx
    lane = _lane_iota((tq, LANE))

    qs = []
    for t in range(3):
        qt = q_ref[:, t * LANE:(t + 1) * LANE]
        for r in range(2):
            qs.append(jnp.where((lane // HEAD_DIM) == r, qt, jnp.zeros_like(qt)))
    qs = jnp.concatenate(qs, axis=0)
    kcat = jnp.concatenate([kl_ref[...], km_ref[...], kr_ref[...], kx_ref[...]], axis=0)
    vcat = jnp.concatenate([vl_ref[...], vm_ref[...], vr_ref[...], vx_ref[...]], axis=0)
    s = lax.dot_general(qs, kcat, (((1,), (1,)), ((), ())), preferred_element_type=F32)

    r_i = lax.broadcasted_iota(jnp.int32, (tq, span), 0)
    c_i = lax.broadcasted_iota(jnp.int32, (tq, span), 1)
    never = 1 << 20
    thr_l = jnp.where(i > 0, 0, never)
    thr_r = jnp.where(i < nq - 1, 0, never)
    ok_l = (c_i >= tq) | ((c_i - r_i) >= thr_l)
    ok_r = (c_i < 2 * tq) | (c_i >= 3 * tq) | ((r_i - (c_i - 2 * tq)) >= thr_r)
    keep = ok_l & ok_r

    outs = []
    for g in range(6):
        t, r = g // 2, g % 2
        sg = jnp.where(keep, s[g * tq:(g + 1) * tq], NEG)
        sk = sink_ref[t + 3 * r] * LOG2E
        m = jnp.maximum(jnp.max(sg, axis=-1, keepdims=True), sk)
        p = jnp.exp2(sg - m)
        l = jnp.sum(p, axis=-1, keepdims=True) + jnp.exp2(sk - m)
        o = jnp.dot(p.astype(BF16), vcat, preferred_element_type=F32) / l
        outs.append(o)
    for t in range(3):
        o_ref[:, t * LANE:(t + 1) * LANE] = jnp.where(lane < HEAD_DIM, outs[2 * t],
                                                      outs[2 * t + 1]).astype(o_ref.dtype)


def _attn_window(proj, ctx_k, ctx_v, sink, out_prev, *, tok_off, n_batch, seq):
    tq = WINDOW
    nq = seq // tq
    blk0 = tok_off // tq
    n_ctx = ctx_k.shape[1]

    def kv_spec(tile, delta):
        def imap(b, i):
            return (blk0 + b * nq + jnp.clip(i + delta, 0, nq - 1), tile)
        return pl.BlockSpec((tq, LANE), imap)

    return pl.pallas_call(
        functools.partial(_attn_window_kernel, tq=tq, nq=nq),
        grid=(n_batch, nq),
        in_specs=[pl.BlockSpec(memory_space=pltpu.SMEM),
                  pl.BlockSpec((tq, 3 * LANE), lambda b, i: (blk0 + b * nq + i, QC_T // 3)),
                  kv_spec(KC_T, -1), kv_spec(KC_T, 0), kv_spec(KC_T, 1),
                  kv_spec(VC_T, -1), kv_spec(VC_T, 0), kv_spec(VC_T, 1),
                  pl.BlockSpec((None, n_ctx, LANE), lambda b, i: (b, 0, 0)),
                  pl.BlockSpec((None, n_ctx, LANE), lambda b, i: (b, 0, 0)),
                  pl.BlockSpec(memory_space=pl.ANY)],
        out_specs=pl.BlockSpec((tq, 3 * LANE), lambda b, i: (blk0 + b * nq + i, 0)),
        out_shape=jax.ShapeDtypeStruct(out_prev.shape, BF16),
        input_output_aliases={10: 0},
        compiler_params=_cparams(("arbitrary", "arbitrary")),
        name="attn_c_window",
    )(sink, proj, proj, proj, proj, proj, proj, proj, ctx_k, ctx_v, out_prev)


def _layer_norm(y, g, b):
    mu = jnp.mean(y, axis=-1, keepdims=True)
    var = jnp.mean(jnp.square(y - mu), axis=-1, keepdims=True)
    return (y - mu) * lax.rsqrt(var + LN_EPS) * g + b


def _outproj_kernel(x_ref, oa_ref, ob_ref, oc_ref, wa_ref, wb_ref, wc_ref, mod_ref, ln_ref,
                    wr_ref, br_ref, x1_ref, h2_ref, gate_ref, idx_ref, *, alpha):
    tm = x_ref.shape[0]
    o = (jnp.dot(oa_ref[...], wa_ref[...], preferred_element_type=F32)
         + jnp.dot(ob_ref[...], wb_ref[...], preferred_element_type=F32)
         + jnp.dot(oc_ref[...], wc_ref[...], preferred_element_type=F32))
    gate1 = mod_ref[2:3, :]
    shift2 = mod_ref[3:4, :]
    scale2 = mod_ref[4:5, :]
    x1 = _layer_norm(alpha * x_ref[...] + gate1 * o, ln_ref[0:1, :], ln_ref[1:2, :])
    x1_ref[...] = x1
    h2 = x1 * (1.0 + scale2) + shift2
    h2_ref[...] = h2

    h_hi = h2.astype(BF16)
    h_lo = (h2 - h_hi.astype(F32)).astype(BF16)
    logits = (jnp.dot(h_hi, wr_ref[0], preferred_element_type=F32)
              + jnp.dot(h_hi, wr_ref[1], preferred_element_type=F32)
              + jnp.dot(h_lo, wr_ref[0], preferred_element_type=F32)
              + br_ref[0:1, :])

    lane = _lane_iota((tm, LANE))
    lane_f = lane.astype(F32)
    cur = logits
    vals, idxs = [], []
    for _ in range(TOP_K):
        mx = jnp.max(cur, axis=-1, keepdims=True)
        ix = jnp.min(jnp.where(cur == mx, lane_f, float(LANE)), axis=-1, keepdims=True)
        vals.append(mx)
        idxs.append(ix)
        cur = jnp.where(lane_f == ix, -jnp.inf, cur)
    es = [jnp.exp(v - vals[0]) for v in vals]
    den = es[0] + es[1] + es[2] + es[3]
    gates = jnp.zeros((tm, LANE), F32)
    idx = jnp.zeros((tm, LANE), F32)
    for k in range(TOP_K):
        gates = jnp.where(lane == k, es[k] / den, gates)
        idx = jnp.where(lane == k, idxs[k], idx)
    gate_ref[...] = gates
    idx_ref[...] = idx.astype(jnp.int32)


def _outproj(x, oa, ob, oc, wa, wb, wc, mod_l, ln, wr, br, *, tm, t_ctx, s_dec, alpha):
    t_all, d = x.shape
    n_ctx_tiles = t_ctx // tm
    tiles_per_seq = s_dec // tm

    def grp(i):
        return jnp.where(i < n_ctx_tiles, 0, 1 + (i - n_ctx_tiles) // tiles_per_seq)

    row = lambda w: pl.BlockSpec((tm, w), lambda i: (i, 0))
    full = lambda a: pl.BlockSpec(a.shape, lambda i: (0,) * a.ndim)
    return pl.pallas_call(
        functools.partial(_outproj_kernel, alpha=alpha),
        grid=(t_all // tm,),
        in_specs=[row(d), row(oa.shape[1]), row(ob.shape[1]), row(oc.shape[1]),
                  full(wa), full(wb), full(wc),
                  pl.BlockSpec((None, 6, d), lambda i: (grp(i), 0, 0)),
                  full(ln), full(wr), full(br)],
        out_specs=[row(d), row(d), row(LANE), row(LANE)],
        out_shape=[jax.ShapeDtypeStruct((t_all, d), F32), jax.ShapeDtypeStruct((t_all, d), F32),
                   jax.ShapeDtypeStruct((t_all, LANE), F32),
                   jax.ShapeDtypeStruct((t_all, LANE), jnp.int32)],
        compiler_params=_cparams(("arbitrary",)),
        name="outproj_ln_router",
    )(x, oa, ob, oc, wa, wb, wc, mod_l, ln, wr, br)


def _row_gather_copy(src_hbm, row, dst, dst_row, sem):
    return pltpu.make_async_copy(src_hbm.at[pl.ds(row, 1)], dst.at[pl.ds(dst_row, 1)], sem)


def _moe_kernel(be_ref, nused_ref, src_cur, src_next, h_hbm, wgu_ref, bgu_ref, wdn_ref, bdn_ref,
                o_ref, xbuf, sems):
    del be_ref
    i = pl.program_id(0)
    n_used = nused_ref[0]
    blk = xbuf.shape[1]

    def issue(src_ref, slot):
        def body(r, carry):
            _row_gather_copy(h_hbm, src_ref[0, r], xbuf.at[slot], r, sems.at[slot]).start()
            return carry
        lax.fori_loop(0, blk, body, 0, unroll=8)

    @pl.when(i == 0)
    def _():
        issue(src_cur, 0)

    @pl.when(i < n_used)
    def _():
        slot = i % 2
        pltpu.make_async_copy(h_hbm.at[pl.ds(0, blk)], xbuf.at[slot], sems.at[slot]).wait()

        @pl.when(i + 1 < n_used)
        def _():
            issue(src_next, 1 - slot)

        x = xbuf[slot].astype(BF16)
        gu = jnp.dot(x, wgu_ref[...], preferred_element_type=F32) + bgu_ref[...]
        d_ff = gu.shape[1] // 2
        glu = jnp.minimum(gu[:, :d_ff], SWIGLU_LIMIT)
        lin = jnp.clip(gu[:, d_ff:], -SWIGLU_LIMIT, SWIGLU_LIMIT)
        act = glu * jax.nn.sigmoid(SWIGLU_ALPHA * glu) * (lin + 1.0)
        o_ref[...] = jnp.dot(act.astype(BF16), wdn_ref[...], preferred_element_type=F32) + bdn_ref[...]

    @pl.when(i >= n_used)
    def _():
        o_ref[...] = jnp.zeros(o_ref.shape, F32)


def _moe(h2, block_e, row_src, n_used, wgu, bgu, wdn, bdn):
    n_blocks = block_e.shape[0]
    d = h2.shape[1]
    n_e, _, n_gu = wgu.shape
    src3 = row_src.reshape(n_blocks, 1, MOE_BLOCK)
    grid_spec = pltpu.PrefetchScalarGridSpec(
        num_scalar_prefetch=2,
        grid=(n_blocks,),
        in_specs=[pl.BlockSpec((None, 1, MOE_BLOCK), lambda i, be, nu: (i, 0, 0),
                               memory_space=pltpu.SMEM),
                  pl.BlockSpec((None, 1, MOE_BLOCK), lambda i, be, nu: (jnp.minimum(i + 1, n_blocks - 1), 0, 0),
                               memory_space=pltpu.SMEM),
                  pl.BlockSpec(memory_space=pl.ANY),
                  pl.BlockSpec((None, d, n_gu), lambda i, be, nu: (be[i], 0, 0)),
                  pl.BlockSpec((None, 1, n_gu), lambda i, be, nu: (be[i], 0, 0)),
                  pl.BlockSpec((None, n_gu // 2, d), lambda i, be, nu: (be[i], 0, 0)),
                  pl.BlockSpec((None, 1, d), lambda i, be, nu: (be[i], 0, 0))],
        out_specs=pl.BlockSpec((MOE_BLOCK, d), lambda i, be, nu: (i, 0)),
        scratch_shapes=[pltpu.VMEM((2, MOE_BLOCK, d), F32), pltpu.SemaphoreType.DMA((2,))],
    )
    return pl.pallas_call(
        _moe_kernel,
        grid_spec=grid_spec,
        out_shape=jax.ShapeDtypeStruct((n_blocks * MOE_BLOCK, d), F32),
        compiler_params=_cparams(("arbitrary",)),
        name="moe_experts",
    )(block_e, n_used, src3, src3, h2, wgu, bgu.reshape(n_e, 1, n_gu), wdn, bdn.reshape(n_e, 1, d))


def _combine_kernel(dst_cur, dst_next, x1_ref, gate_ref, mod_ref, ln_ref, y_hbm, o_ref, gbuf, sems,
                    *, alpha):
    i = pl.program_id(0)
    n = pl.num_programs(0)
    tm = x1_ref.shape[0]

    def issue(dst_ref, slot):
        def body(t, carry):
            for k in range(TOP_K):
                _row_gather_copy(y_hbm, dst_ref[0, t * TOP_K + k], gbuf.at[slot, k], t,
                                 sems.at[slot]).start()
            return carry
        lax.fori_loop(0, tm, body, 0, unroll=4)

    @pl.when(i == 0)
    def _():
        issue(dst_cur, 0)

    slot = i % 2
    for k in range(TOP_K):
        pltpu.make_async_copy(y_hbm.at[pl.ds(0, tm)], gbuf.at[slot, k], sems.at[slot]).wait()

    @pl.when(i + 1 < n)
    def _():
        issue(dst_next, 1 - slot)

    gates = gate_ref[...]
    y = gates[:, 0:1] * gbuf[slot, 0]
    for k in range(1, TOP_K):
        y = y + gates[:, k:k + 1] * gbuf[slot, k]
    gate2 = mod_ref[5:6, :]
    o_ref[...] = _layer_norm(alpha * x1_ref[...] + gate2 * y, ln_ref[0:1, :], ln_ref[1:2, :])


def _combine(x1, gates, dest, y_sorted, mod_l, ln, *, tm, t_ctx, s_dec, alpha):
    t_all, d = x1.shape
    n_ctx_tiles = t_ctx // tm
    tiles_per_seq = s_dec // tm

    def grp(i):
        return jnp.where(i < n_ctx_tiles, 0, 1 + (i - n_ctx_tiles) // tiles_per_seq)

    n_tiles = t_all // tm
    dst3 = dest.reshape(n_tiles, 1, tm * TOP_K)
    return pl.pallas_call(
        functools.partial(_combine_kernel, alpha=alpha),
        grid=(n_tiles,),
        in_specs=[pl.BlockSpec((None, 1, tm * TOP_K), lambda i: (i, 0, 0), memory_space=pltpu.SMEM),
                  pl.BlockSpec((None, 1, tm * TOP_K), lambda i: (jnp.minimum(i + 1, n_tiles - 1), 0, 0),
                               memory_space=pltpu.SMEM),
                  pl.BlockSpec((tm, d), lambda i: (i, 0)),
                  pl.BlockSpec((tm, LANE), lambda i: (i, 0)),
                  pl.BlockSpec((None, 6, d), lambda i: (grp(i), 0, 0)),
                  pl.BlockSpec(ln.shape, lambda i: (0, 0)),
                  pl.BlockSpec(memory_space=pl.ANY)],
        out_specs=pl.BlockSpec((tm, d), lambda i: (i, 0)),
        out_shape=jax.ShapeDtypeStruct((t_all, d), F32),
        scratch_shapes=[pltpu.VMEM((2, TOP_K, tm, d), F32), pltpu.SemaphoreType.DMA((2,))],
        compiler_params=_cparams(("arbitrary",)),
        name="combine_ln",
    )(dst3, dst3, x1, gates, mod_l, ln, y_sorted)


def _routing(idx):
    n_tok = idx.shape[0]
    n_asg = n_tok * TOP_K
    flat_e = idx.reshape(-1)
    order = jnp.argsort(flat_e).astype(jnp.int32)
    counts = jnp.sum((flat_e[:, None] == jnp.arange(N_EXPERTS)[None, :]).astype(jnp.int32), axis=0)
    starts = jnp.cumsum(counts) - counts
    padded = (counts + MOE_BLOCK - 1) // MOE_BLOCK * MOE_BLOCK
    pends = jnp.cumsum(padded)
    pstarts = pends - padded
    n_blocks = -(-n_asg // MOE_BLOCK) + N_EXPERTS
    n_rows = n_blocks * MOE_BLOCK
    sorted_e = flat_e[order]
    dest_sorted = (pstarts[sorted_e] + jnp.arange(n_asg, dtype=jnp.int32) - starts[sorted_e]).astype(jnp.int32)
    dest = jnp.zeros((n_asg,), jnp.int32).at[order].set(dest_sorted)
    rows = jnp.arange(n_rows, dtype=jnp.int32)
    row_e = jnp.minimum(jnp.searchsorted(pends, rows, side='right'), N_EXPERTS - 1).astype(jnp.int32)
    rank = rows - pstarts[row_e]
    valid = rank < counts[row_e]
    src_pos = jnp.clip(starts[row_e] + rank, 0, n_asg - 1)
    row_src = jnp.where(valid, order[src_pos] // TOP_K, 0).astype(jnp.int32)
    block_e = row_e[::MOE_BLOCK]
    n_used = (pends[-1] // MOE_BLOCK).astype(jnp.int32).reshape(1)
    return block_e, row_src, n_used, dest


def _in_perm():
    qa, ka, va = 0, 256, 512
    qb, kb, vb = 768, 1152, 1280
    qc, kc, vc = 1408, 1792, 1920
    cols = []
    for base in (qb, qc):
        for h in GQA_ORDER:
            cols += list(range(base + h * HEAD_DIM, base + (h + 1) * HEAD_DIM))
    for base, width in ((qa, 256), (ka, 256), (kb, 128), (kc, 128), (va, 256), (vb, 128), (vc, 128)):
        cols += list(range(base, base + width))
    return jnp.asarray(cols, jnp.int32)


def _gqa_rows(base):
    rows = []
    for h in GQA_ORDER:
        rows += list(range(base + h * HEAD_DIM, base + (h + 1) * HEAD_DIM))
    return jnp.asarray(rows, jnp.int32)


def _rope_patterns(rows, tm):
    def tables(dim):
        row = jnp.repeat(jnp.arange(rows), GRID_W).astype(F32)
        col = jnp.tile(jnp.arange(GRID_W), rows).astype(F32)
        nf = dim // 4
        freqs = ROPE_THETA ** (-jnp.arange(nf, dtype=F32) / nf)
        ang = jnp.concatenate([row[:, None] * freqs, col[:, None] * freqs], -1)
        return jnp.cos(ang), jnp.sin(ang)

    out = []
    for dim in (A_QK_DIM, HEAD_DIM):
        cos, sin = tables(dim)
        nf = dim // 4
        cr, cc, sr, sc = cos[:, :nf], cos[:, nf:], sin[:, :nf], sin[:, nf:]
        cpat = jnp.concatenate([cr, cr, cc, cc], -1)
        spat = jnp.concatenate([-sr, sr, -sc, sc], -1)
        reps = LANE // dim
        out += [jnp.tile(cpat, (1, reps)), jnp.tile(spat, (1, reps))]
    tab = jnp.stack(out, 0)
    ident = jnp.stack([jnp.ones((tm, LANE), F32), jnp.zeros((tm, LANE), F32)] * 2, 0)
    return jnp.concatenate([tab, ident], axis=1)


def _ctx_cache(cache):
    b, l, p, h, d = cache.shape
    return jnp.transpose(cache, (1, 0, 2, 3, 4)).reshape(l, b, p, h * d).astype(BF16)


def kernel(x_prompt, x_sample, cache_a_k, cache_a_v, cache_b_k, cache_b_v, cache_c_k, cache_c_v, c, c_ctx, w_mod, b_mod, w_in, a_lambda, a_subln_g, b_q_norm_g, b_k_norm_g, c_sink, w_out, ln1_g, ln1_b, w_router, b_router, w_gate_up, b_gate_up, w_down, b_down, ln2_g, ln2_b):
    depth = w_in.shape[0]
    n_ctx_b, s_ctx, d = x_prompt.shape
    n_dec_b, s_dec, _ = x_sample.shape
    t_ctx = n_ctx_b * s_ctx
    t_dec = n_dec_b * s_dec
    t_all = t_ctx + t_dec
    alpha = (2 * depth) ** 0.25
    tm = 512 if (t_ctx % 512 == 0 and s_dec % 512 == 0) else 256
    tm_c = 256

    x = jnp.concatenate([x_prompt.reshape(t_ctx, d), x_sample.reshape(t_dec, d)], axis=0)

    n_grp = 1 + n_dec_b
    g_pad = -(-n_grp // 8) * 8
    cond = jnp.zeros((g_pad, d), F32).at[0].set(c_ctx).at[1:n_grp].set(c)
    mod = _modulation(cond, w_mod, b_mod).reshape(depth, g_pad, 6, d)

    perm = _in_perm()
    w_in_p = jnp.take(w_in, perm, axis=2).astype(BF16)
    wa = w_out[:, 0:256].astype(BF16)
    wb = jnp.take(w_out, _gqa_rows(256), axis=1).astype(BF16)
    wc = jnp.take(w_out, _gqa_rows(640), axis=1).astype(BF16)
    wgu = w_gate_up.astype(BF16)
    wdn = w_down.astype(BF16)

    pad_e = LANE - N_EXPERTS
    wr_f = jnp.pad(w_router, ((0, 0), (0, 0), (0, pad_e)))
    wr_hi = wr_f.astype(BF16)
    wr_lo = (wr_f - wr_hi.astype(F32)).astype(BF16)
    wr = jnp.stack([wr_hi, wr_lo], axis=1)
    br = jnp.pad(b_router, ((0, 0), (0, pad_e)), constant_values=NEG).reshape(depth, 1, LANE)
    br = jnp.broadcast_to(br, (depth, 8, LANE))

    rope_tab = _rope_patterns(s_dec // GRID_W, tm)
    tile2 = lambda g: jnp.tile(g, (1, LANE // HEAD_DIM))
    gains = jnp.zeros((depth, 8, LANE), F32).at[:, 0].set(tile2(b_q_norm_g)).at[:, 1].set(tile2(b_k_norm_g))
    subln = jnp.broadcast_to(tile2(a_subln_g)[:, None, :], (depth, 8, LANE))
    ln1 = jnp.stack([ln1_g, ln1_b], axis=1)
    ln2 = jnp.stack([ln2_g, ln2_b], axis=1)

    xa_k, xa_v = _ctx_cache(cache_a_k), _ctx_cache(cache_a_v)
    xb_k, xb_v = _ctx_cache(cache_b_k), _ctx_cache(cache_b_v)
    xc_k, xc_v = _ctx_cache(cache_c_k), _ctx_cache(cache_c_v)

    tq_ctx = min(256, s_ctx)
    tq_dec = 256
    kv_layers = []
    for l in range(depth):
        lam0 = _lambda_init(l)
        mod_l = mod[l]
        proj, kv = _inproj(x, mod_l, w_in_p[l], rope_tab, gains[l], tm=tm, t_ctx=t_ctx, s_dec=s_dec)
        kv_layers.append(kv)

        common_a = dict(mode="diff", q_tile=QA_T, k_tile=KA_T, v_tile=VA_T, a_lambda=a_lambda[l],
                        subln_g=subln[l], lam_init=lam0, out_width=A_HEADS * HEAD_DIM)
        oa = _attn_full(proj, tok_off=0, n_batch=n_ctx_b, seq=s_ctx, tq=tq_ctx, name="attn_a_ctx", **common_a)
        oa = _attn_full(proj, tok_off=t_ctx, n_batch=n_dec_b, seq=s_dec, tq=tq_dec, ctx_k=xa_k[l],
                        ctx_v=xa_v[l], out_prev=oa, name="attn_a_dec", **common_a)

        common_b = dict(mode="gqa", q_tile=QB_T, k_tile=KB_T, v_tile=VB_T, out_width=B_HEADS * HEAD_DIM)
        ob = _attn_full(proj, tok_off=0, n_batch=n_ctx_b, seq=s_ctx, tq=tq_ctx // 2, name="attn_b_ctx", **common_b)
        ob = _attn_full(proj, tok_off=t_ctx, n_batch=n_dec_b, seq=s_dec, tq=tq_dec // 2, ctx_k=xb_k[l],
                        ctx_v=xb_v[l], out_prev=ob, name="attn_b_dec", **common_b)

        sink_p = c_sink[l]
        oc = _attn_full(proj, mode="gqa", tok_off=0, n_batch=n_ctx_b, seq=s_ctx, tq=tq_ctx // 2,
                        q_tile=QC_T, k_tile=KC_T, v_tile=VC_T, sink=sink_p,
                        out_width=C_HEADS * HEAD_DIM, name="attn_c_ctx")
        oc = _attn_window(proj, xc_k[l], xc_v[l], sink_p, oc, tok_off=t_ctx, n_batch=n_dec_b, seq=s_dec)

        x1, h2, gates, idx = _outproj(x, oa, ob, oc, wa[l], wb[l], wc[l], mod_l, ln1[l], wr[l], br[l],
                                      tm=tm, t_ctx=t_ctx, s_dec=s_dec, alpha=alpha)
        block_e, row_src, n_used, dest = _routing(idx[:, :TOP_K])
        y_sorted = _moe(h2, block_e, row_src, n_used, wgu[l], b_gate_up[l], wdn[l], b_down[l])
        x = _combine(x1, gates, dest, y_sorted, mod_l, ln2[l], tm=tm_c, t_ctx=t_ctx, s_dec=s_dec, alpha=alpha)

    y = x[:t_ctx].reshape(n_ctx_b, s_ctx, d)
    z = x[t_ctx:].reshape(n_dec_b, s_dec, d)
    kv_all = jnp.stack(kv_layers, axis=1).reshape(n_ctx_b, s_ctx, depth, 8 * LANE)
    kv_all = jnp.transpose(kv_all, (0, 2, 1, 3))

    def cache_out(lo, width, heads):
        return kv_all[..., lo:lo + width].reshape(n_ctx_b, depth, s_ctx, heads, width // heads)

    new_a_k = cache_out(0, 256, A_HEADS)
    new_b_k = cache_out(256, 128, B_KV_HEADS)
    new_c_k = cache_out(384, 128, C_KV_HEADS)
    new_a_v = cache_out(512, 256, A_HEADS)
    new_b_v = cache_out(768, 128, B_KV_HEADS)
    new_c_v = cache_out(896, 128, C_KV_HEADS)
    return (y, z, new_a_k, new_a_v, new_b_k, new_b_v, new_c_k, new_c_v)
```

```python
import functools
import math

import jax
import jax.numpy as jnp
from jax import lax
from jax.experimental import pallas as pl
from jax.experimental.pallas import tpu as pltpu

F32 = jnp.float32
BF16 = jnp.bfloat16

LANE = 128
HEAD_DIM = 64
A_HEADS = 4
A_QK_DIM = 32
B_HEADS = 6
B_KV_HEADS = 2
C_HEADS = 6
C_KV_HEADS = 2
GRID_W = 64
WINDOW = 128
ROPE_THETA = 10000.0
N_EXPERTS = 32
TOP_K = 4
SWIGLU_ALPHA = 1.702
SWIGLU_LIMIT = 7.0
MOE_BLOCK = 256
LN_EPS = 1e-5
RMS_EPS = 1e-6
NEG = -1e30
LOG2E = math.log2(math.e)
VMEM_LIMIT = 56 * 1024 * 1024

QB_T, QC_T, QA_T, KA_T, KB_T, KC_T, VA_T, VB_T, VC_T = 0, 3, 6, 8, 10, 11, 12, 14, 15
GQA_ORDER = (0, 3, 1, 4, 2, 5)


def _lambda_init(layer):
    return 0.8 - 0.6 * math.exp(-0.3 * layer)


def _lane_iota(shape):
    return lax.broadcasted_iota(jnp.int32, shape, len(shape) - 1)


def _cparams(sem, vmem=VMEM_LIMIT):
    return pltpu.CompilerParams(dimension_semantics=sem, vmem_limit_bytes=vmem)


def _mod_kernel(c_ref, w_ref, b_ref, o_ref):
    c = c_ref[...]
    s = (c * jax.nn.sigmoid(c)).astype(BF16)
    o_ref[...] = jnp.dot(s, w_ref[...].astype(BF16), preferred_element_type=F32) + b_ref[...]


def _modulation(cond, w_mod, b_mod):
    depth, d, n = w_mod.shape
    g = cond.shape[0]
    tn = 1536
    return pl.pallas_call(
        _mod_kernel,
        grid=(depth, n // tn),
        in_specs=[pl.BlockSpec((g, d), lambda l, j: (0, 0)),
                  pl.BlockSpec((None, d, tn), lambda l, j: (l, 0, j)),
                  pl.BlockSpec((None, 1, tn), lambda l, j: (l, 0, j))],
        out_specs=pl.BlockSpec((None, g, tn), lambda l, j: (l, 0, j)),
        out_shape=jax.ShapeDtypeStruct((depth, g, n), F32),
        compiler_params=_cparams(("arbitrary", "arbitrary")),
        name="modulation",
    )(cond, w_mod, b_mod.reshape(depth, 1, n))


def _swap_blocks(x, blk, lane):
    up = pltpu.roll(x, LANE - blk, 1)
    dn = pltpu.roll(x, blk, 1)
    return jnp.where((lane % (2 * blk)) < blk, up, dn)


def _segment_mean_sq(x, ones_seg):
    sq = x * x
    hi = sq.astype(BF16)
    lo = (sq - hi.astype(F32)).astype(BF16)
    tot = (jnp.dot(hi, ones_seg, preferred_element_type=F32)
           + jnp.dot(lo, ones_seg, preferred_element_type=F32))
    return tot * (1.0 / HEAD_DIM)


def _inproj_kernel(x_ref, mod_ref, w_ref, rope_ref, g_ref, proj_ref, kv_ref, *, n_ctx_tiles):
    i = pl.program_id(0)
    tm = x_ref.shape[0]
    x = x_ref[...]
    shift1 = mod_ref[0:1, :]
    scale1 = mod_ref[1:2, :]
    h = (x * (1.0 + scale1) + shift1).astype(BF16)

    lane = _lane_iota((tm, LANE))
    r_i = lax.broadcasted_iota(jnp.int32, (LANE, LANE), 0)
    c_i = lax.broadcasted_iota(jnp.int32, (LANE, LANE), 1)
    ones_seg = jnp.where((r_i // HEAD_DIM) == (c_i // HEAD_DIM), 1.0, 0.0).astype(BF16)

    cos_a, sin_a, cos_b, sin_b = rope_ref[0], rope_ref[1], rope_ref[2], rope_ref[3]
    gq = g_ref[0:1, :]
    gk = g_ref[1:2, :]
    qa_scale = (A_QK_DIM ** -0.5) * LOG2E
    q_scale = (HEAD_DIM ** -0.5) * LOG2E

    def section(t0, nt):
        return jnp.dot(h, w_ref[:, t0 * LANE:(t0 + nt) * LANE], preferred_element_type=F32)

    def tile(sec, t):
        return sec[:, t * LANE:(t + 1) * LANE]

    def rope(xt, cos, sin, blk):
        return xt * cos + _swap_blocks(xt, blk, lane) * sin

    def rms(xt, g):
        return xt * lax.rsqrt(_segment_mean_sq(xt, ones_seg) + RMS_EPS) * g

    def put(t, val):
        proj_ref[:, t * LANE:(t + 1) * LANE] = val.astype(BF16)

    kv_tiles = []

    sec = section(QB_T, 3)
    for t in range(3):
        put(QB_T + t, rope(rms(tile(sec, t), gq), cos_b, sin_b, 16) * q_scale)
    sec = section(QC_T, 3)
    for t in range(3):
        put(QC_T + t, rope(tile(sec, t), cos_b, sin_b, 16) * q_scale)
    sec = section(QA_T, 4)
    for t in range(2):
        put(QA_T + t, rope(tile(sec, t), cos_a, sin_a, 8) * qa_scale)
    for t in range(2):
        ka = rope(tile(sec, 2 + t), cos_a, sin_a, 8)
        put(KA_T + t, ka)
        kv_tiles.append(ka)
    sec = section(KB_T, 2)
    kb = rope(rms(tile(sec, 0), gk), cos_b, sin_b, 16)
    put(KB_T, kb)
    kc = rope(tile(sec, 1), cos_b, sin_b, 16)
    put(KC_T, kc)
    kv_tiles += [kb, kc]
    sec = section(VA_T, 4)
    for t in range(4):
        put(VA_T + t, tile(sec, t))
        kv_tiles.append(tile(sec, t))

    @pl.when(i < n_ctx_tiles)
    def _():
        for t, val in enumerate(kv_tiles):
            kv_ref[:, t * LANE:(t + 1) * LANE] = val


def _inproj(x, mod_l, w_in_p, rope_tab, gains, *, tm, t_ctx, s_dec):
    t_all, d = x.shape
    n_ctx_tiles = t_ctx // tm
    tiles_per_seq = s_dec // tm
    s_rope_tiles = (rope_tab.shape[1] - tm) // tm

    def grp(i):
        return jnp.where(i < n_ctx_tiles, 0, 1 + (i - n_ctx_tiles) // tiles_per_seq)

    def rope_blk(i):
        return jnp.where(i < n_ctx_tiles, s_rope_tiles, (i - n_ctx_tiles) % tiles_per_seq)

    n_out = w_in_p.shape[1]
    return pl.pallas_call(
        functools.partial(_inproj_kernel, n_ctx_tiles=n_ctx_tiles),
        grid=(t_all // tm,),
        in_specs=[pl.BlockSpec((tm, d), lambda i: (i, 0)),
                  pl.BlockSpec((None, 6, d), lambda i: (grp(i), 0, 0)),
                  pl.BlockSpec((d, n_out), lambda i: (0, 0)),
                  pl.BlockSpec((4, tm, LANE), lambda i: (0, rope_blk(i), 0)),
                  pl.BlockSpec((8, LANE), lambda i: (0, 0))],
        out_specs=[pl.BlockSpec((tm, n_out), lambda i: (i, 0)),
                   pl.BlockSpec((tm, 8 * LANE), lambda i: (jnp.minimum(i, n_ctx_tiles - 1), 0))],
        out_shape=[jax.ShapeDtypeStruct((t_all, n_out), BF16),
                   jax.ShapeDtypeStruct((t_ctx, 8 * LANE), F32)],
        compiler_params=_cparams(("arbitrary",)),
        name="inproj",
    )(x, mod_l, w_in_p, rope_tab, gains)


def _attn_full_kernel(*refs, mode, nt, tq, n_new, n_ctx, has_sink, lam_init):
    refs = list(refs)
    sink_ref = refs.pop(0) if has_sink else None
    q_ref, k_ref, v_ref = refs[:3]
    refs = refs[3:]
    if n_ctx:
        kx_ref, vx_ref = refs[:2]
        refs = refs[2:]
    if mode == "diff":
        alam_ref, g_ref = refs[:2]
        refs = refs[2:]
    o_ref, qs_ref, s_ref, p_ref, l_ref = refs

    n_grp = 4 if mode == "diff" else 2
    gw = LANE // n_grp
    rows = nt * n_grp * tq
    lane = _lane_iota((tq, LANE))

    rb = min(128, tq)
    rc = min(256, rows)
    nb = tq // rb

    if mode == "diff":
        lane_b = _lane_iota((rb, LANE))
        for hh in range(2):
            for b in range(nb):
                for mp in range(2):
                    row0 = ((hh * nb + b) * 2 + mp) * rb
                    qs_ref[row0:row0 + rb, :] = jnp.where(
                        (lane_b // gw) == 2 * hh + mp, q_ref[b * rb:(b + 1) * rb, :],
                        jnp.zeros((rb, LANE), BF16))
    else:
        for t in range(nt):
            qt = q_ref[:, t * LANE:(t + 1) * LANE]
            for r in range(n_grp):
                row0 = (t * n_grp + r) * tq
                qs_ref[row0:row0 + tq, :] = jnp.where((lane // gw) == r, qt, jnp.zeros_like(qt))

    kt = min(512, n_new)
    tiles = [(k_ref, c * kt) for c in range(n_new // kt)]
    if n_ctx:
        tiles += [(kx_ref, c * kt) for c in range(n_ctx // kt)]

    def scores(r0, n):
        for c, (ref, off) in enumerate(tiles):
            s_ref[r0:r0 + n, c * kt:(c + 1) * kt] = lax.dot_general(
                qs_ref[r0:r0 + n, :], ref[off:off + kt, :], (((1,), (1,)), ((), ())),
                preferred_element_type=F32)

    nl = (n_new + n_ctx) // LANE

    def row_max(r0):
        mp = s_ref[r0:r0 + rb, 0:LANE]
        for c in range(1, nl):
            mp = jnp.maximum(mp, s_ref[r0:r0 + rb, c * LANE:(c + 1) * LANE])
        return jnp.max(mp, axis=-1, keepdims=True)

    def exp_pass(r0, m, keep):
        mb = jnp.broadcast_to(m, (rb, LANE))
        lp = None
        for c in range(nl):
            p = jnp.exp2(s_ref[r0:r0 + rb, c * LANE:(c + 1) * LANE] - mb)
            lp = p if lp is None else lp + p
            keep(c, p)
        return jnp.sum(lp, axis=-1, keepdims=True)

    def values(r0, n):
        acc = jnp.dot(p_ref[r0:r0 + n, :n_new], v_ref[...], preferred_element_type=F32)
        if n_ctx:
            acc = acc + jnp.dot(p_ref[r0:r0 + n, n_new:], vx_ref[...], preferred_element_type=F32)
        return acc

    if mode == "diff":
        a = alam_ref[...]
        lam = (jnp.exp(jnp.sum(a[0:1, :] * a[1:2, :], axis=-1, keepdims=True))
               - jnp.exp(jnp.sum(a[2:3, :] * a[3:4, :], axis=-1, keepdims=True)) + lam_init)
        heads = []
        for hh in range(2):
            for b in range(nb):
                r1 = (hh * nb + b) * 2 * rb
                r2 = r1 + rb
                b0 = b * rb
                scores(r1, 2 * rb)
                ls = []
                for r0 in (r1, r2):
                    def keep_f32(c, p, r0=r0):
                        s_ref[r0:r0 + rb, c * LANE:(c + 1) * LANE] = p
                    ls.append(exp_pass(r0, row_max(r0), keep_f32))
                w1 = jnp.broadcast_to(1.0 / ls[0], (rb, LANE))
                w2 = jnp.broadcast_to(lam / ls[1], (rb, LANE))
                o0 = hh * tq + b0
                for c in range(nl):
                    cs = slice(c * LANE, (c + 1) * LANE)
                    pc = s_ref[r1:r1 + rb, cs] * w1 - s_ref[r2:r2 + rb, cs] * w2
                    p_ref[o0:o0 + rb, cs] = pc.astype(BF16)
            heads.append(values(hh * tq, tq))
        out = jnp.where(lane < HEAD_DIM, heads[0], heads[1])
        sq = out * out
        ms_lo = jnp.sum(jnp.where(lane < HEAD_DIM, sq, 0.0), axis=-1, keepdims=True)
        ms_hi = jnp.sum(jnp.where(lane < HEAD_DIM, 0.0, sq), axis=-1, keepdims=True)
        ms = jnp.where(lane < HEAD_DIM, ms_lo, ms_hi) * (1.0 / HEAD_DIM)
        out = out * lax.rsqrt(ms + RMS_EPS) * g_ref[0:1, :] * (1.0 - lam_init)
        o_ref[...] = out.astype(o_ref.dtype)
        return

    accs = []
    for c0 in range(0, rows, rc):
        scores(c0, rc)
        for r0 in range(c0, c0 + rc, rb):
            m = row_max(r0)
            if has_sink:
                g = r0 // tq
                sk = sink_ref[(g // n_grp) + 3 * (g % n_grp)] * LOG2E
                m = jnp.maximum(m, sk)

            def keep_bf16(c, p, r0=r0):
                p_ref[r0:r0 + rb, c * LANE:(c + 1) * LANE] = p.astype(BF16)
            l = exp_pass(r0, m, keep_bf16)
            if has_sink:
                l = l + jnp.exp2(sk - m)
            l_ref[r0:r0 + rb, :] = l
        accs.append(values(c0, rc))
    o = jnp.concatenate(accs, axis=0) / l_ref[...]
    for t in range(nt):
        lo = o[(2 * t) * tq:(2 * t + 1) * tq]
        hi = o[(2 * t + 1) * tq:(2 * t + 2) * tq]
        o_ref[:, t * LANE:(t + 1) * LANE] = jnp.where(lane < HEAD_DIM, lo, hi).astype(o_ref.dtype)


def _attn_full(proj, *, mode, tok_off, n_batch, seq, tq, q_tile, k_tile, v_tile,
               ctx_k=None, ctx_v=None, ctx_tile=0, sink=None, a_lambda=None, subln_g=None,
               lam_init=0.0, out_prev=None, out_width=None, name="attn"):
    t_all = proj.shape[0]
    nt = 1 if mode == "diff" else 3
    n_pair = 2 if mode == "diff" else 1
    n_grp = 4 if mode == "diff" else 2
    rows = nt * n_grp * tq
    nq = seq // tq
    qblk0 = tok_off // tq
    sblk0 = tok_off // seq
    n_ctx = 0 if ctx_k is None else ctx_k.shape[1]
    n_keys = seq + n_ctx
    qw = nt * LANE
    qcol0 = q_tile // nt

    args, in_specs = [], []
    if sink is not None:
        args.append(sink)
        in_specs.append(pl.BlockSpec(memory_space=pltpu.SMEM))
    args += [proj, proj, proj]
    in_specs += [
        pl.BlockSpec((tq, qw), lambda b, j, i: (qblk0 + b * nq + i, qcol0 + j)),
        pl.BlockSpec((seq, LANE), lambda b, j, i: (sblk0 + b, k_tile + j)),
        pl.BlockSpec((seq, LANE), lambda b, j, i: (sblk0 + b, v_tile + j)),
    ]
    if n_ctx:
        args += [ctx_k, ctx_v]
        in_specs += [pl.BlockSpec((None, n_ctx, LANE), lambda b, j, i: (b, 0, ctx_tile + j)),
                     pl.BlockSpec((None, n_ctx, LANE), lambda b, j, i: (b, 0, ctx_tile + j))]
    if mode == "diff":
        args += [a_lambda, subln_g]
        in_specs += [pl.BlockSpec(a_lambda.shape, lambda b, j, i: (0, 0)),
                     pl.BlockSpec(subln_g.shape, lambda b, j, i: (0, 0))]
    aliases = {}
    if out_prev is not None:
        aliases = {len(args): 0}
        args.append(out_prev)
        in_specs.append(pl.BlockSpec(memory_space=pl.ANY))

    kern = functools.partial(_attn_full_kernel, mode=mode, nt=nt, tq=tq, n_new=seq,
                             n_ctx=n_ctx, has_sink=sink is not None, lam_init=lam_init)

    def wrapped(*refs):
        if out_prev is not None:
            n_in = len(args)
            refs = refs[:n_in - 1] + refs[n_in:]
        kern(*refs)

    return pl.pallas_call(
        wrapped,
        grid=(n_batch, n_pair, nq),
        in_specs=in_specs,
        out_specs=pl.BlockSpec((tq, qw), lambda b, j, i: (qblk0 + b * nq + i, j)),
        out_shape=jax.ShapeDtypeStruct((t_all, out_width), BF16),
        scratch_shapes=[pltpu.VMEM((rows, LANE), BF16), pltpu.VMEM((rows, n_keys), F32),
                        pltpu.VMEM((rows // 2 if mode == "diff" else rows, n_keys), BF16),
                        pltpu.VMEM((rows, 1), F32)],
        input_output_aliases=aliases,
        compiler_params=_cparams(("arbitrary", "arbitrary", "arbitrary")),
        name=name,
    )(*args)


def _attn_window_kernel(sink_ref, q_ref, kl_ref, km_ref, kr_ref, vl_ref, vm_ref, vr_ref,
                        kx_ref, vx_ref, prev_ref, o_ref, *, tq, nq):
    del prev_ref
    i = pl.program_id(1)
    n_ctx = kx_ref.shape[0]
    span = 3 * tq + n_ctx
    lane = _lane_iota((tq, LANE))

    qs = []
    for t in range(3):
        qt = q_ref[:, t * LANE:(t + 1) * LANE]
        for r in range(2):
            qs.append(jnp.where((lane // HEAD_DIM) == r, qt, jnp.zeros_like(qt)))
    qs = jnp.concatenate(qs, axis=0)
    kcat = jnp.concatenate([kl_ref[...], km_ref[...], kr_ref[...], kx_ref[...]], axis=0)
    vcat = jnp.concatenate([vl_ref[...], vm_ref[...], vr_ref[...], vx_ref[...]], axis=0)
    s = lax.dot_general(qs, kcat, (((1,), (1,)), ((), ())), preferred_element_type=F32)

    r_i = lax.broadcasted_iota(jnp.int32, (tq, span), 0)
    c_i = lax.broadcasted_iota(jnp.int32, (tq, span), 1)
    never = 1 << 20
    thr_l = jnp.where(i > 0, 0, never)
    thr_r = jnp.where(i < nq - 1, 0, never)
    ok_l = (c_i >= tq) | ((c_i - r_i) >= thr_l)
    ok_r = (c_i < 2 * tq) | (c_i >= 3 * tq) | ((r_i - (c_i - 2 * tq)) >= thr_r)
    keep = ok_l & ok_r

    outs = []
    for g in range(6):
        t, r = g // 2, g % 2
        sg = jnp.where(keep, s[g * tq:(g + 1) * tq], NEG)
        sk = sink_ref[t + 3 * r] * LOG2E
        m = jnp.maximum(jnp.max(sg, axis=-1, keepdims=True), sk)
        p = jnp.exp2(sg - m)
        l = jnp.sum(p, axis=-1, keepdims=True) + jnp.exp2(sk - m)
        o = jnp.dot(p.astype(BF16), vcat, preferred_element_type=F32) / l
        outs.append(o)
    for t in range(3):
        o_ref[:, t * LANE:(t + 1) * LANE] = jnp.where(lane < HEAD_DIM, outs[2 * t],
                                                      outs[2 * t + 1]).astype(o_ref.dtype)


def _attn_window(proj, ctx_k, ctx_v, sink, out_prev, *, tok_off, n_batch, seq):
    tq = WINDOW
    nq = seq // tq
    blk0 = tok_off // tq
    n_ctx = ctx_k.shape[1]

    def kv_spec(tile, delta):
        def imap(b, i):
            return (blk0 + b * nq + jnp.clip(i + delta, 0, nq - 1), tile)
        return pl.BlockSpec((tq, LANE), imap)

    return pl.pallas_call(
        functools.partial(_attn_window_kernel, tq=tq, nq=nq),
        grid=(n_batch, nq),
        in_specs=[pl.BlockSpec(memory_space=pltpu.SMEM),
                  pl.BlockSpec((tq, 3 * LANE), lambda b, i: (blk0 + b * nq + i, QC_T // 3)),
                  kv_spec(KC_T, -1), kv_spec(KC_T, 0), kv_spec(KC_T, 1),
                  kv_spec(VC_T, -1), kv_spec(VC_T, 0), kv_spec(VC_T, 1),
                  pl.BlockSpec((None, n_ctx, LANE), lambda b, i: (b, 0, 0)),
                  pl.BlockSpec((None, n_ctx, LANE), lambda b, i: (b, 0, 0)),
                  pl.BlockSpec(memory_space=pl.ANY)],
        out_specs=pl.BlockSpec((tq, 3 * LANE), lambda b, i: (blk0 + b * nq + i, 0)),
        out_shape=jax.ShapeDtypeStruct(out_prev.shape, BF16),
        input_output_aliases={10: 0},
        compiler_params=_cparams(("arbitrary", "arbitrary")),
        name="attn_c_window",
    )(sink, proj, proj, proj, proj, proj, proj, proj, ctx_k, ctx_v, out_prev)


def _layer_norm(y, g, b):
    mu = jnp.mean(y, axis=-1, keepdims=True)
    var = jnp.mean(jnp.square(y - mu), axis=-1, keepdims=True)
    return (y - mu) * lax.rsqrt(var + LN_EPS) * g + b


def _outproj_kernel(x_ref, oa_ref, ob_ref, oc_ref, wa_ref, wb_ref, wc_ref, mod_ref, ln_ref,
                    wr_ref, br_ref, x1_ref, h2_ref, gate_ref, idx_ref, *, alpha):
    tm = x_ref.shape[0]
    o = (jnp.dot(oa_ref[...], wa_ref[...], preferred_element_type=F32)
         + jnp.dot(ob_ref[...], wb_ref[...], preferred_element_type=F32)
         + jnp.dot(oc_ref[...], wc_ref[...], preferred_element_type=F32))
    gate1 = mod_ref[2:3, :]
    shift2 = mod_ref[3:4, :]
    scale2 = mod_ref[4:5, :]
    x1 = _layer_norm(alpha * x_ref[...] + gate1 * o, ln_ref[0:1, :], ln_ref[1:2, :])
    x1_ref[...] = x1
    h2 = x1 * (1.0 + scale2) + shift2
    h2_ref[...] = h2

    h_hi = h2.astype(BF16)
    h_lo = (h2 - h_hi.astype(F32)).astype(BF16)
    logits = (jnp.dot(h_hi, wr_ref[0], preferred_element_type=F32)
              + jnp.dot(h_hi, wr_ref[1], preferred_element_type=F32)
              + jnp.dot(h_lo, wr_ref[0], preferred_element_type=F32)
              + br_ref[0:1, :])

    lane = _lane_iota((tm, LANE))
    lane_f = lane.astype(F32)
    cur = logits
    vals, idxs = [], []
    for _ in range(TOP_K):
        mx = jnp.max(cur, axis=-1, keepdims=True)
        ix = jnp.min(jnp.where(cur == mx, lane_f, float(LANE)), axis=-1, keepdims=True)
        vals.append(mx)
        idxs.append(ix)
        cur = jnp.where(lane_f == ix, -jnp.inf, cur)
    es = [jnp.exp(v - vals[0]) for v in vals]
    den = es[0] + es[1] + es[2] + es[3]
    gates = jnp.zeros((tm, LANE), F32)
    idx = jnp.zeros((tm, LANE), F32)
    for k in range(TOP_K):
        gates = jnp.where(lane == k, es[k] / den, gates)
        idx = jnp.where(lane == k, idxs[k], idx)
    gate_ref[...] = gates
    idx_ref[...] = idx.astype(jnp.int32)


def _outproj(x, oa, ob, oc, wa, wb, wc, mod_l, ln, wr, br, *, tm, t_ctx, s_dec, alpha):
    t_all, d = x.shape
    n_ctx_tiles = t_ctx // tm
    tiles_per_seq = s_dec // tm

    def grp(i):
        return jnp.where(i < n_ctx_tiles, 0, 1 + (i - n_ctx_tiles) // tiles_per_seq)

    row = lambda w: pl.BlockSpec((tm, w), lambda i: (i, 0))
    full = lambda a: pl.BlockSpec(a.shape, lambda i: (0,) * a.ndim)
    return pl.pallas_call(
        functools.partial(_outproj_kernel, alpha=alpha),
        grid=(t_all // tm,),
        in_specs=[row(d), row(oa.shape[1]), row(ob.shape[1]), row(oc.shape[1]),
                  full(wa), full(wb), full(wc),
                  pl.BlockSpec((None, 6, d), lambda i: (grp(i), 0, 0)),
                  full(ln), full(wr), full(br)],
        out_specs=[row(d), row(d), row(LANE), row(LANE)],
        out_shape=[jax.ShapeDtypeStruct((t_all, d), F32), jax.ShapeDtypeStruct((t_all, d), F32),
                   jax.ShapeDtypeStruct((t_all, LANE), F32),
                   jax.ShapeDtypeStruct((t_all, LANE), jnp.int32)],
        compiler_params=_cparams(("arbitrary",)),
        name="outproj_ln_router",
    )(x, oa, ob, oc, wa, wb, wc, mod_l, ln, wr, br)


def _moe_kernel(be_ref, nused_ref, src_first, src_next, dst_prev, h_hbm, wgu_ref, bgu_ref, wdn_ref,
                bdn_ref, y_hbm, xbuf, ybuf, gsem, ssem, *, prime_row0):
    del be_ref
    i = pl.program_id(0)
    n_used = nused_ref[0]
    blk = xbuf.shape[1]

    def gather(src_ref, r, slot):
        return pltpu.make_async_copy(h_hbm.at[pl.ds(src_ref[0, r], 1)], xbuf.at[slot, pl.ds(r, 1)],
                                     gsem.at[slot])

    def scatter(row, r, slot):
        return pltpu.make_async_copy(ybuf.at[slot, pl.ds(r, 1)], y_hbm.at[pl.ds(row, 1)], ssem.at[slot])

    def wait_gather(slot):
        pltpu.make_async_copy(h_hbm.at[pl.ds(0, blk)], xbuf.at[slot], gsem.at[slot]).wait()

    def wait_scatter(slot):
        pltpu.make_async_copy(ybuf.at[slot], y_hbm.at[pl.ds(0, blk)], ssem.at[slot]).wait()

    @pl.when(i == 0)
    def _():
        ybuf[...] = jnp.zeros(ybuf.shape, F32)

        def body(r, carry):
            gather(src_first, r, 0).start()
            scatter(prime_row0 + r, r, 0).start()
            return carry
        lax.fori_loop(0, blk, body, 0)

    def compute_block(slot):
        wait_gather(slot)
        wait_scatter(slot)
        x = xbuf[slot].astype(BF16)
        for r in range(blk):
            gather(src_next, r, 1 - slot).start()
            scatter(dst_prev[0, r], r, 1 - slot).start()
        gu = jnp.dot(x, wgu_ref[...], preferred_element_type=F32) + bgu_ref[...]
        d_ff = gu.shape[1] // 2
        glu = jnp.minimum(gu[:, :d_ff], SWIGLU_LIMIT)
        lin = jnp.clip(gu[:, d_ff:], -SWIGLU_LIMIT, SWIGLU_LIMIT)
        act = glu * jax.nn.sigmoid(SWIGLU_ALPHA * glu) * (lin + 1.0)
        ybuf[slot] = jnp.dot(act.astype(BF16), wdn_ref[...], preferred_element_type=F32) + bdn_ref[...]

    for parity in range(2):
        pl.when((i < n_used) & (i % 2 == parity))(functools.partial(compute_block, parity))

    @pl.when(i == n_used)
    def _():
        slot = i % 2
        wait_gather(slot)
        wait_scatter(slot)

        def body(r, carry):
            scatter(dst_prev[0, r], r, 1 - slot).start()
            return carry
        lax.fori_loop(0, blk, body, 0)
        wait_scatter(1 - slot)


def _moe(h2, block_e, row_src, row_dst, n_used, wgu, bgu, wdn, bdn, *, n_out_rows, prime_row0):
    n_blocks = block_e.shape[0]
    d = h2.shape[1]
    n_e, _, n_gu = wgu.shape
    idx_spec = lambda imap: pl.BlockSpec((None, 1, MOE_BLOCK), imap, memory_space=pltpu.SMEM)
    be = lambda i, b: b[jnp.minimum(i, n_blocks - 1)]
    grid_spec = pltpu.PrefetchScalarGridSpec(
        num_scalar_prefetch=2,
        grid=(n_blocks + 1,),
        in_specs=[idx_spec(lambda i, b, nu: (0, 0, 0)),
                  idx_spec(lambda i, b, nu: (jnp.minimum(i + 1, n_blocks), 0, 0)),
                  idx_spec(lambda i, b, nu: (i, 0, 0)),
                  pl.BlockSpec(memory_space=pl.ANY),
                  pl.BlockSpec((None, d, n_gu), lambda i, b, nu: (be(i, b), 0, 0)),
                  pl.BlockSpec((None, 1, n_gu), lambda i, b, nu: (be(i, b), 0, 0)),
                  pl.BlockSpec((None, n_gu // 2, d), lambda i, b, nu: (be(i, b), 0, 0)),
                  pl.BlockSpec((None, 1, d), lambda i, b, nu: (be(i, b), 0, 0))],
        out_specs=pl.BlockSpec(memory_space=pl.ANY),
        scratch_shapes=[pltpu.VMEM((2, MOE_BLOCK, d), F32), pltpu.VMEM((2, MOE_BLOCK, d), F32),
                        pltpu.SemaphoreType.DMA((2,)), pltpu.SemaphoreType.DMA((2,))],
    )
    return pl.pallas_call(
        functools.partial(_moe_kernel, prime_row0=prime_row0),
        grid_spec=grid_spec,
        out_shape=jax.ShapeDtypeStruct((n_out_rows, d), F32),
        compiler_params=_cparams(("arbitrary",)),
        name="moe_experts",
    )(block_e, n_used, row_src, row_src, row_dst, h2, wgu, bgu.reshape(n_e, 1, n_gu), wdn,
      bdn.reshape(n_e, 1, d))


def _combine_kernel(x1_ref, gate_ref, mod_ref, ln_ref, y0_ref, y1_ref, y2_ref, y3_ref, o_ref, *, alpha):
    gates = gate_ref[...]
    y = gates[:, 0:1] * y0_ref[...]
    for k, y_ref in enumerate((y1_ref, y2_ref, y3_ref), start=1):
        y = y + gates[:, k:k + 1] * y_ref[...]
    gate2 = mod_ref[5:6, :]
    o_ref[...] = _layer_norm(alpha * x1_ref[...] + gate2 * y, ln_ref[0:1, :], ln_ref[1:2, :])


def _combine(x1, gates, y4, mod_l, ln, *, tm, t_ctx, s_dec, alpha):
    t_all, d = x1.shape
    n_ctx_tiles = t_ctx // tm
    tiles_per_seq = s_dec // tm
    n_tiles = t_all // tm

    def grp(i):
        return jnp.where(i < n_ctx_tiles, 0, 1 + (i - n_ctx_tiles) // tiles_per_seq)

    slab = lambda k: pl.BlockSpec((tm, d), lambda i: (k * n_tiles + i, 0))
    return pl.pallas_call(
        functools.partial(_combine_kernel, alpha=alpha),
        grid=(n_tiles,),
        in_specs=[pl.BlockSpec((tm, d), lambda i: (i, 0)),
                  pl.BlockSpec((tm, LANE), lambda i: (i, 0)),
                  pl.BlockSpec((None, 6, d), lambda i: (grp(i), 0, 0)),
                  pl.BlockSpec(ln.shape, lambda i: (0, 0)),
                  slab(0), slab(1), slab(2), slab(3)],
        out_specs=pl.BlockSpec((tm, d), lambda i: (i, 0)),
        out_shape=jax.ShapeDtypeStruct((t_all, d), F32),
        compiler_params=_cparams(("arbitrary",)),
        name="combine_ln",
    )(x1, gates, mod_l, ln, y4, y4, y4, y4)


def _routing(idx):
    n_tok = idx.shape[0]
    n_asg = n_tok * TOP_K
    flat_e = idx.reshape(-1)
    order = jnp.argsort(flat_e).astype(jnp.int32)
    experts = jnp.arange(N_EXPERTS, dtype=jnp.int32)
    counts = jnp.sum((flat_e[:, None] == experts[None, :]).astype(jnp.int32), axis=0)
    starts = jnp.cumsum(counts) - counts
    padded = (counts + MOE_BLOCK - 1) // MOE_BLOCK * MOE_BLOCK
    pends = jnp.cumsum(padded)
    pstarts = pends - padded
    n_blocks = -(-n_asg // MOE_BLOCK) + N_EXPERTS
    n_rows = n_blocks * MOE_BLOCK

    rows = jnp.arange(n_rows, dtype=jnp.int32)
    row_e = jnp.minimum(jnp.sum((rows[:, None] >= pends[None, :]).astype(jnp.int32), axis=1), N_EXPERTS - 1)
    onehot = (row_e[:, None] == experts[None, :]).astype(jnp.int32)
    rank = rows - jnp.sum(onehot * pstarts[None, :], axis=1)
    valid = rank < jnp.sum(onehot * counts[None, :], axis=1)
    src_pos = jnp.clip(jnp.sum(onehot * starts[None, :], axis=1) + rank, 0, n_asg - 1)
    asg = order[src_pos]
    tok = asg // TOP_K
    spare0 = n_asg
    row_src = jnp.where(valid, tok, 0).astype(jnp.int32)
    row_dst = jnp.where(valid, (asg % TOP_K) * n_tok + tok, spare0 + rows).astype(jnp.int32)
    lead = spare0 + n_rows + jnp.arange(MOE_BLOCK, dtype=jnp.int32)
    row_src = jnp.concatenate([row_src, jnp.zeros((MOE_BLOCK,), jnp.int32)]).reshape(n_blocks + 1, 1, MOE_BLOCK)
    row_dst = jnp.concatenate([lead, row_dst]).reshape(n_blocks + 1, 1, MOE_BLOCK)
    block_e = row_e[::MOE_BLOCK]
    n_used = (pends[-1] // MOE_BLOCK).astype(jnp.int32).reshape(1)
    prime_row0 = spare0 + n_rows + MOE_BLOCK
    n_out_rows = prime_row0 + MOE_BLOCK
    return block_e, row_src, row_dst, n_used, n_out_rows, prime_row0


def _in_perm():
    qa, ka, va = 0, 256, 512
    qb, kb, vb = 768, 1152, 1280
    qc, kc, vc = 1408, 1792, 1920
    cols = []
    for base in (qb, qc):
        for h in GQA_ORDER:
            cols += list(range(base + h * HEAD_DIM, base + (h + 1) * HEAD_DIM))
    for base, width in ((qa, 256), (ka, 256), (kb, 128), (kc, 128), (va, 256), (vb, 128), (vc, 128)):
        cols += list(range(base, base + width))
    return jnp.asarray(cols, jnp.int32)


def _gqa_rows(base):
    rows = []
    for h in GQA_ORDER:
        rows += list(range(base + h * HEAD_DIM, base + (h + 1) * HEAD_DIM))
    return jnp.asarray(rows, jnp.int32)


def _rope_patterns(rows, tm):
    def tables(dim):
        row = jnp.repeat(jnp.arange(rows), GRID_W).astype(F32)
        col = jnp.tile(jnp.arange(GRID_W), rows).astype(F32)
        nf = dim // 4
        freqs = ROPE_THETA ** (-jnp.arange(nf, dtype=F32) / nf)
        ang = jnp.concatenate([row[:, None] * freqs, col[:, None] * freqs], -1)
        return jnp.cos(ang), jnp.sin(ang)

    out = []
    for dim in (A_QK_DIM, HEAD_DIM):
        cos, sin = tables(dim)
        nf = dim // 4
        cr, cc, sr, sc = cos[:, :nf], cos[:, nf:], sin[:, :nf], sin[:, nf:]
        cpat = jnp.concatenate([cr, cr, cc, cc], -1)
        spat = jnp.concatenate([-sr, sr, -sc, sc], -1)
        reps = LANE // dim
        out += [jnp.tile(cpat, (1, reps)), jnp.tile(spat, (1, reps))]
    tab = jnp.stack(out, 0)
    ident = jnp.stack([jnp.ones((tm, LANE), F32), jnp.zeros((tm, LANE), F32)] * 2, 0)
    return jnp.concatenate([tab, ident], axis=1)


def _ctx_cache(cache):
    b, l, p, h, d = cache.shape
    return jnp.transpose(cache, (1, 0, 2, 3, 4)).reshape(l, b, p, h * d).astype(BF16)


def kernel(x_prompt, x_sample, cache_a_k, cache_a_v, cache_b_k, cache_b_v, cache_c_k, cache_c_v, c, c_ctx, w_mod, b_mod, w_in, a_lambda, a_subln_g, b_q_norm_g, b_k_norm_g, c_sink, w_out, ln1_g, ln1_b, w_router, b_router, w_gate_up, b_gate_up, w_down, b_down, ln2_g, ln2_b):
    depth = w_in.shape[0]
    n_ctx_b, s_ctx, d = x_prompt.shape
    n_dec_b, s_dec, _ = x_sample.shape
    t_ctx = n_ctx_b * s_ctx
    t_dec = n_dec_b * s_dec
    t_all = t_ctx + t_dec
    alpha = (2 * depth) ** 0.25
    tm = 512 if (t_ctx % 512 == 0 and s_dec % 512 == 0) else 256

    x = jnp.concatenate([x_prompt.reshape(t_ctx, d), x_sample.reshape(t_dec, d)], axis=0)

    n_grp = 1 + n_dec_b
    g_pad = -(-n_grp // 8) * 8
    cond = jnp.zeros((g_pad, d), F32).at[0].set(c_ctx).at[1:n_grp].set(c)
    mod = _modulation(cond, w_mod, b_mod).reshape(depth, g_pad, 6, d)

    perm = _in_perm()
    w_in_p = jnp.take(w_in, perm, axis=2).astype(BF16)
    wa = w_out[:, 0:256].astype(BF16)
    wb = jnp.take(w_out, _gqa_rows(256), axis=1).astype(BF16)
    wc = jnp.take(w_out, _gqa_rows(640), axis=1).astype(BF16)
    wgu = w_gate_up.astype(BF16)
    wdn = w_down.astype(BF16)

    pad_e = LANE - N_EXPERTS
    wr_f = jnp.pad(w_router, ((0, 0), (0, 0), (0, pad_e)))
    wr_hi = wr_f.astype(BF16)
    wr_lo = (wr_f - wr_hi.astype(F32)).astype(BF16)
    wr = jnp.stack([wr_hi, wr_lo], axis=1)
    br = jnp.pad(b_router, ((0, 0), (0, pad_e)), constant_values=NEG).reshape(depth, 1, LANE)
    br = jnp.broadcast_to(br, (depth, 8, LANE))

    rope_tab = _rope_patterns(s_dec // GRID_W, tm)
    tile2 = lambda g: jnp.tile(g, (1, LANE // HEAD_DIM))
    gains = jnp.zeros((depth, 8, LANE), F32).at[:, 0].set(tile2(b_q_norm_g)).at[:, 1].set(tile2(b_k_norm_g))
    subln = jnp.broadcast_to(tile2(a_subln_g)[:, None, :], (depth, 8, LANE))
    ln1 = jnp.stack([ln1_g, ln1_b], axis=1)
    ln2 = jnp.stack([ln2_g, ln2_b], axis=1)

    xa_k, xa_v = _ctx_cache(cache_a_k), _ctx_cache(cache_a_v)
    xb_k, xb_v = _ctx_cache(cache_b_k), _ctx_cache(cache_b_v)
    xc_k, xc_v = _ctx_cache(cache_c_k), _ctx_cache(cache_c_v)

    tq_ctx = min(256, s_ctx)
    tq_dec = 256
    kv_layers = []
    for l in range(depth):
        lam0 = _lambda_init(l)
        mod_l = mod[l]
        proj, kv = _inproj(x, mod_l, w_in_p[l], rope_tab, gains[l], tm=tm, t_ctx=t_ctx, s_dec=s_dec)
        kv_layers.append(kv)

        common_a = dict(mode="diff", q_tile=QA_T, k_tile=KA_T, v_tile=VA_T, a_lambda=a_lambda[l],
                        subln_g=subln[l], lam_init=lam0, out_width=A_HEADS * HEAD_DIM)
        oa = _attn_full(proj, tok_off=0, n_batch=n_ctx_b, seq=s_ctx, tq=tq_ctx, name="attn_a_ctx", **common_a)
        oa = _attn_full(proj, tok_off=t_ctx, n_batch=n_dec_b, seq=s_dec, tq=tq_dec, ctx_k=xa_k[l],
                        ctx_v=xa_v[l], out_prev=oa, name="attn_a_dec", **common_a)

        common_b = dict(mode="gqa", q_tile=QB_T, k_tile=KB_T, v_tile=VB_T, out_width=B_HEADS * HEAD_DIM)
        ob = _attn_full(proj, tok_off=0, n_batch=n_ctx_b, seq=s_ctx, tq=tq_ctx // 2, name="attn_b_ctx", **common_b)
        ob = _attn_full(proj, tok_off=t_ctx, n_batch=n_dec_b, seq=s_dec, tq=tq_dec // 2, ctx_k=xb_k[l],
                        ctx_v=xb_v[l], out_prev=ob, name="attn_b_dec", **common_b)

        sink_p = c_sink[l]
        oc = _attn_full(proj, mode="gqa", tok_off=0, n_batch=n_ctx_b, seq=s_ctx, tq=tq_ctx // 2,
                        q_tile=QC_T, k_tile=KC_T, v_tile=VC_T, sink=sink_p,
                        out_width=C_HEADS * HEAD_DIM, name="attn_c_ctx")
        oc = _attn_window(proj, xc_k[l], xc_v[l], sink_p, oc, tok_off=t_ctx, n_batch=n_dec_b, seq=s_dec)

        x1, h2, gates, idx = _outproj(x, oa, ob, oc, wa[l], wb[l], wc[l], mod_l, ln1[l], wr[l], br[l],
                                      tm=tm, t_ctx=t_ctx, s_dec=s_dec, alpha=alpha)
        block_e, row_src, row_dst, n_used, n_out_rows, prime_row0 = _routing(idx[:, :TOP_K])
        y4 = _moe(h2, block_e, row_src, row_dst, n_used, wgu[l], b_gate_up[l], wdn[l], b_down[l],
                  n_out_rows=n_out_rows, prime_row0=prime_row0)
        x = _combine(x1, gates, y4, mod_l, ln2[l], tm=tm, t_ctx=t_ctx, s_dec=s_dec, alpha=alpha)

    y = x[:t_ctx].reshape(n_ctx_b, s_ctx, d)
    z = x[t_ctx:].reshape(n_dec_b, s_dec, d)
    kv_all = jnp.stack(kv_layers, axis=1).reshape(n_ctx_b, s_ctx, depth, 8 * LANE)
    kv_all = jnp.transpose(kv_all, (0, 2, 1, 3))

    def cache_out(lo, width, heads):
        return kv_all[..., lo:lo + width].reshape(n_ctx_b, depth, s_ctx, heads, width // heads)

    new_a_k = cache_out(0, 256, A_HEADS)
    new_b_k = cache_out(256, 128, B_KV_HEADS)
    new_c_k = cache_out(384, 128, C_KV_HEADS)
    new_a_v = cache_out(512, 256, A_HEADS)
    new_b_v = cache_out(768, 128, B_KV_HEADS)
    new_c_v = cache_out(896, 128, C_KV_HEADS)
    return (y, z, new_a_k, new_a_v, new_b_k, new_b_v, new_c_k, new_c_v)
```

```python
import functools
import math

import jax
import jax.numpy as jnp
from jax import lax
from jax.experimental import pallas as pl
from jax.experimental.pallas import tpu as pltpu

F32 = jnp.float32
BF16 = jnp.bfloat16

LANE = 128
TOKEN_TILE = 8
HEAD_DIM = 64
A_HEADS = 4
A_QK_DIM = 32
B_HEADS = 6
B_KV_HEADS = 2
C_HEADS = 6
C_KV_HEADS = 2
GRID_W = 64
WINDOW = 128
ROPE_THETA = 10000.0
N_EXPERTS = 32
TOP_K = 4
SWIGLU_ALPHA = 1.702
SWIGLU_LIMIT = 7.0
MOE_BLOCK = 256
LN_EPS = 1e-5
RMS_EPS = 1e-6
NEG = -1e30
LOG2E = math.log2(math.e)
VMEM_LIMIT = 56 * 1024 * 1024

QB_T, QC_T, QA_T, KA_T, KB_T, KC_T, VA_T, VB_T, VC_T = 0, 3, 6, 8, 10, 11, 12, 14, 15
GQA_ORDER = (0, 3, 1, 4, 2, 5)


def _lambda_init(layer):
    return 0.8 - 0.6 * math.exp(-0.3 * layer)


def _lane_iota(shape):
    return lax.broadcasted_iota(jnp.int32, shape, len(shape) - 1)


def _cparams(sem, vmem=VMEM_LIMIT):
    return pltpu.CompilerParams(dimension_semantics=sem, vmem_limit_bytes=vmem)


def _mod_kernel(c_ref, w_ref, b_ref, o_ref):
    c = c_ref[...]
    s = (c * jax.nn.sigmoid(c)).astype(BF16)
    o_ref[...] = jnp.dot(s, w_ref[...].astype(BF16), preferred_element_type=F32) + b_ref[...]


def _modulation(cond, w_mod, b_mod):
    depth, d, n = w_mod.shape
    g = cond.shape[0]
    tn = 1536
    return pl.pallas_call(
        _mod_kernel,
        grid=(depth, n // tn),
        in_specs=[pl.BlockSpec((g, d), lambda l, j: (0, 0)),
                  pl.BlockSpec((None, d, tn), lambda l, j: (l, 0, j)),
                  pl.BlockSpec((None, 1, tn), lambda l, j: (l, 0, j))],
        out_specs=pl.BlockSpec((None, g, tn), lambda l, j: (l, 0, j)),
        out_shape=jax.ShapeDtypeStruct((depth, g, n), F32),
        compiler_params=_cparams(("arbitrary", "arbitrary")),
        name="modulation",
    )(cond, w_mod, b_mod.reshape(depth, 1, n))


def _swap_blocks(x, blk, lane):
    up = pltpu.roll(x, LANE - blk, 1)
    dn = pltpu.roll(x, blk, 1)
    return jnp.where((lane % (2 * blk)) < blk, up, dn)


def _segment_mean_sq(x, ones_seg):
    sq = x * x
    hi = sq.astype(BF16)
    lo = (sq - hi.astype(F32)).astype(BF16)
    tot = (jnp.dot(hi, ones_seg, preferred_element_type=F32)
           + jnp.dot(lo, ones_seg, preferred_element_type=F32))
    return tot * (1.0 / HEAD_DIM)


def _inproj_kernel(x_ref, mod_ref, w_ref, rope_ref, g_ref, proj_ref, kv_ref, *, n_ctx_tiles):
    i = pl.program_id(0)
    tm = x_ref.shape[0]
    x = x_ref[...]
    shift1 = mod_ref[0:1, :]
    scale1 = mod_ref[1:2, :]
    h = (x * (1.0 + scale1) + shift1).astype(BF16)

    lane = _lane_iota((tm, LANE))
    r_i = lax.broadcasted_iota(jnp.int32, (LANE, LANE), 0)
    c_i = lax.broadcasted_iota(jnp.int32, (LANE, LANE), 1)
    ones_seg = jnp.where((r_i // HEAD_DIM) == (c_i // HEAD_DIM), 1.0, 0.0).astype(BF16)

    cos_a, sin_a, cos_b, sin_b = rope_ref[0], rope_ref[1], rope_ref[2], rope_ref[3]
    gq = g_ref[0:1, :]
    gk = g_ref[1:2, :]
    qa_scale = (A_QK_DIM ** -0.5) * LOG2E
    q_scale = (HEAD_DIM ** -0.5) * LOG2E

    def section(t0, nt):
        return jnp.dot(h, w_ref[:, t0 * LANE:(t0 + nt) * LANE], preferred_element_type=F32)

    def tile(sec, t):
        return sec[:, t * LANE:(t + 1) * LANE]

    def rope(xt, cos, sin, blk):
        return xt * cos + _swap_blocks(xt, blk, lane) * sin

    def rms(xt, g):
        return xt * lax.rsqrt(_segment_mean_sq(xt, ones_seg) + RMS_EPS) * g

    def put(t, val):
        proj_ref[:, t * LANE:(t + 1) * LANE] = val.astype(BF16)

    kv_tiles = []

    sec = section(QB_T, 3)
    for t in range(3):
        put(QB_T + t, rope(rms(tile(sec, t), gq), cos_b, sin_b, 16) * q_scale)
    sec = section(QC_T, 3)
    for t in range(3):
        put(QC_T + t, rope(tile(sec, t), cos_b, sin_b, 16) * q_scale)
    sec = section(QA_T, 4)
    for t in range(2):
        put(QA_T + t, rope(tile(sec, t), cos_a, sin_a, 8) * qa_scale)
    for t in range(2):
        ka = rope(tile(sec, 2 + t), cos_a, sin_a, 8)
        put(KA_T + t, ka)
        kv_tiles.append(ka)
    sec = section(KB_T, 2)
    kb = rope(rms(tile(sec, 0), gk), cos_b, sin_b, 16)
    put(KB_T, kb)
    kc = rope(tile(sec, 1), cos_b, sin_b, 16)
    put(KC_T, kc)
    kv_tiles += [kb, kc]
    sec = section(VA_T, 4)
    for t in range(4):
        put(VA_T + t, tile(sec, t))
        kv_tiles.append(tile(sec, t))

    @pl.when(i < n_ctx_tiles)
    def _():
        for t, val in enumerate(kv_tiles):
            kv_ref[:, t * LANE:(t + 1) * LANE] = val


def _inproj(x, mod_l, w_in_p, rope_tab, gains, *, tm, t_ctx, s_dec):
    t_all, d = x.shape
    n_ctx_tiles = t_ctx // tm
    tiles_per_seq = s_dec // tm
    s_rope_tiles = (rope_tab.shape[1] - tm) // tm

    def grp(i):
        return jnp.where(i < n_ctx_tiles, 0, 1 + (i - n_ctx_tiles) // tiles_per_seq)

    def rope_blk(i):
        return jnp.where(i < n_ctx_tiles, s_rope_tiles, (i - n_ctx_tiles) % tiles_per_seq)

    n_out = w_in_p.shape[1]
    return pl.pallas_call(
        functools.partial(_inproj_kernel, n_ctx_tiles=n_ctx_tiles),
        grid=(t_all // tm,),
        in_specs=[pl.BlockSpec((tm, d), lambda i: (i, 0)),
                  pl.BlockSpec((None, 6, d), lambda i: (grp(i), 0, 0)),
                  pl.BlockSpec((d, n_out), lambda i: (0, 0)),
                  pl.BlockSpec((4, tm, LANE), lambda i: (0, rope_blk(i), 0)),
                  pl.BlockSpec((8, LANE), lambda i: (0, 0))],
        out_specs=[pl.BlockSpec((tm, n_out), lambda i: (i, 0)),
                   pl.BlockSpec((tm, 8 * LANE), lambda i: (jnp.minimum(i, n_ctx_tiles - 1), 0))],
        out_shape=[jax.ShapeDtypeStruct((t_all, n_out), BF16),
                   jax.ShapeDtypeStruct((t_ctx, 8 * LANE), F32)],
        compiler_params=_cparams(("arbitrary",)),
        name="inproj",
    )(x, mod_l, w_in_p, rope_tab, gains)


def _attn_full_kernel(*refs, mode, nt, tq, n_new, n_ctx, has_sink, lam_init):
    refs = list(refs)
    sink_ref = refs.pop(0) if has_sink else None
    q_ref, k_ref, v_ref = refs[:3]
    refs = refs[3:]
    if n_ctx:
        kx_ref, vx_ref = refs[:2]
        refs = refs[2:]
    if mode == "diff":
        alam_ref, g_ref = refs[:2]
        refs = refs[2:]
        o_ref, qs_ref, s0, s1, f0, f1, p0, p1 = refs
        f_bufs = (f0, f1)
    else:
        o_ref, qs_ref, s0, s1, p0, p1 = refs
    s_bufs, p_bufs = (s0, s1), (p0, p1)

    n_grp = 4 if mode == "diff" else 2
    gw = LANE // n_grp
    rows = nt * n_grp * tq
    lane = _lane_iota((tq, LANE))

    rb = min(128, tq)
    rc = min(256, rows)
    nb = tq // rb

    if mode == "diff":
        lane_b = _lane_iota((rb, LANE))
        for hh in range(2):
            for b in range(nb):
                for mp in range(2):
                    row0 = ((hh * nb + b) * 2 + mp) * rb
                    qs_ref[row0:row0 + rb, :] = jnp.where(
                        (lane_b // gw) == 2 * hh + mp, q_ref[b * rb:(b + 1) * rb, :],
                        jnp.zeros((rb, LANE), BF16))
    else:
        for t in range(nt):
            qt = q_ref[:, t * LANE:(t + 1) * LANE]
            for r in range(n_grp):
                row0 = (t * n_grp + r) * tq
                qs_ref[row0:row0 + tq, :] = jnp.where((lane // gw) == r, qt, jnp.zeros_like(qt))

    kt = min(512, n_new)
    tiles = [(k_ref, c * kt) for c in range(n_new // kt)]
    if n_ctx:
        tiles += [(kx_ref, c * kt) for c in range(n_ctx // kt)]

    def at(off, n):
        return slice(off, off + n)

    def tile_cols(c):
        return slice(c * LANE, (c + 1) * LANE)

    def scores(s_buf, q0, n):
        for c, (ref, off) in enumerate(tiles):
            s_buf[at(0, n), c * kt:(c + 1) * kt] = lax.dot_general(
                qs_ref[q0:q0 + n, :], ref[off:off + kt, :], (((1,), (1,)), ((), ())),
                preferred_element_type=F32)

    nl = (n_new + n_ctx) // LANE

    def row_max(s_buf, off):
        mp = s_buf[at(off, rb), tile_cols(0)]
        for c in range(1, nl):
            mp = jnp.maximum(mp, s_buf[at(off, rb), tile_cols(c)])
        return jnp.max(mp, axis=-1, keepdims=True)

    def exp_pass(s_buf, off, m, keep):
        mb = jnp.broadcast_to(m, (rb, LANE))
        lp = None
        for c in range(nl):
            p = jnp.exp2(s_buf[at(off, rb), tile_cols(c)] - mb)
            lp = p if lp is None else lp + p
            keep(c, p)
        return jnp.sum(lp, axis=-1, keepdims=True)

    def values(p_buf, n):
        acc = jnp.dot(p_buf[at(0, n), :n_new], v_ref[...], preferred_element_type=F32)
        if n_ctx:
            acc = acc + jnp.dot(p_buf[at(0, n), n_new:], vx_ref[...], preferred_element_type=F32)
        return acc

    if mode == "diff":
        a = alam_ref[...]
        lam = (jnp.exp(jnp.sum(a[0:1, :] * a[1:2, :], axis=-1, keepdims=True))
               - jnp.exp(jnp.sum(a[2:3, :] * a[3:4, :], axis=-1, keepdims=True)) + lam_init)
        heads = []
        for hh in range(2):
            for b in range(nb):
                k = hh * nb + b
                s_buf, f_buf, p_buf = s_bufs[k % 2], f_bufs[k % 2], p_bufs[hh]
                scores(s_buf, k * 2 * rb, 2 * rb)
                ls = []
                for off in (0, rb):
                    def keep_f32(c, p, off=off, f_buf=f_buf):
                        f_buf[at(off, rb), tile_cols(c)] = p
                    ls.append(exp_pass(s_buf, off, row_max(s_buf, off), keep_f32))
                w1 = jnp.broadcast_to(1.0 / ls[0], (rb, LANE))
                w2 = jnp.broadcast_to(lam / ls[1], (rb, LANE))
                for c in range(nl):
                    pc = f_buf[at(0, rb), tile_cols(c)] * w1 - f_buf[at(rb, rb), tile_cols(c)] * w2
                    p_buf[at(b * rb, rb), tile_cols(c)] = pc.astype(BF16)
            heads.append(values(p_bufs[hh], tq))
        out = jnp.where(lane < HEAD_DIM, heads[0], heads[1])
        sq = out * out
        ms_lo = jnp.sum(jnp.where(lane < HEAD_DIM, sq, 0.0), axis=-1, keepdims=True)
        ms_hi = jnp.sum(jnp.where(lane < HEAD_DIM, 0.0, sq), axis=-1, keepdims=True)
        ms = jnp.where(lane < HEAD_DIM, ms_lo, ms_hi) * (1.0 / HEAD_DIM)
        out = out * lax.rsqrt(ms + RMS_EPS) * g_ref[0:1, :] * (1.0 - lam_init)
        o_ref[...] = out.astype(o_ref.dtype)
        return

    accs, ls = [], []
    for ci, c0 in enumerate(range(0, rows, rc)):
        s_buf, p_buf = s_bufs[ci % 2], p_bufs[ci % 2]
        scores(s_buf, c0, rc)
        for off in range(0, rc, rb):
            m = row_max(s_buf, off)
            if has_sink:
                g = (c0 + off) // tq
                sk = sink_ref[(g // n_grp) + 3 * (g % n_grp)] * LOG2E
                m = jnp.maximum(m, sk)

            def keep_bf16(c, p, off=off, p_buf=p_buf):
                p_buf[at(off, rb), tile_cols(c)] = p.astype(BF16)
            l = exp_pass(s_buf, off, m, keep_bf16)
            if has_sink:
                l = l + jnp.exp2(sk - m)
            ls.append(l)
        accs.append(values(p_buf, rc))
    o = jnp.concatenate(accs, axis=0) / jnp.concatenate(ls, axis=0)
    for t in range(nt):
        lo = o[(2 * t) * tq:(2 * t + 1) * tq]
        hi = o[(2 * t + 1) * tq:(2 * t + 2) * tq]
        o_ref[:, t * LANE:(t + 1) * LANE] = jnp.where(lane < HEAD_DIM, lo, hi).astype(o_ref.dtype)


def _attn_full(proj, *, mode, tok_off, n_batch, seq, tq, q_tile, k_tile, v_tile,
               ctx_k=None, ctx_v=None, ctx_tile=0, sink=None, a_lambda=None, subln_g=None,
               lam_init=0.0, out_prev=None, out_width=None, name="attn"):
    t_all = proj.shape[0]
    nt = 1 if mode == "diff" else 3
    n_pair = 2 if mode == "diff" else 1
    n_grp = 4 if mode == "diff" else 2
    rows = nt * n_grp * tq
    nq = seq // tq
    qblk0 = tok_off // tq
    sblk0 = tok_off // seq
    n_ctx = 0 if ctx_k is None else ctx_k.shape[1]
    n_keys = seq + n_ctx
    qw = nt * LANE
    qcol0 = q_tile // nt

    args, in_specs = [], []
    if sink is not None:
        args.append(sink)
        in_specs.append(pl.BlockSpec(memory_space=pltpu.SMEM))
    args += [proj, proj, proj]
    in_specs += [
        pl.BlockSpec((tq, qw), lambda b, j, i: (qblk0 + b * nq + i, qcol0 + j)),
        pl.BlockSpec((seq, LANE), lambda b, j, i: (sblk0 + b, k_tile + j)),
        pl.BlockSpec((seq, LANE), lambda b, j, i: (sblk0 + b, v_tile + j)),
    ]
    if n_ctx:
        args += [ctx_k, ctx_v]
        in_specs += [pl.BlockSpec((None, n_ctx, LANE), lambda b, j, i: (b, 0, ctx_tile + j)),
                     pl.BlockSpec((None, n_ctx, LANE), lambda b, j, i: (b, 0, ctx_tile + j))]
    if mode == "diff":
        args += [a_lambda, subln_g]
        in_specs += [pl.BlockSpec(a_lambda.shape, lambda b, j, i: (0, 0)),
                     pl.BlockSpec(subln_g.shape, lambda b, j, i: (0, 0))]
    aliases = {}
    if out_prev is not None:
        aliases = {len(args): 0}
        args.append(out_prev)
        in_specs.append(pl.BlockSpec(memory_space=pl.ANY))

    kern = functools.partial(_attn_full_kernel, mode=mode, nt=nt, tq=tq, n_new=seq,
                             n_ctx=n_ctx, has_sink=sink is not None, lam_init=lam_init)

    rb = min(128, tq)
    if mode == "diff":
        scratch = ([pltpu.VMEM((rows, LANE), BF16)]
                   + [pltpu.VMEM((2 * rb, n_keys), F32)] * 4
                   + [pltpu.VMEM((tq, n_keys), BF16)] * 2)
    else:
        rc = min(256, rows)
        scratch = ([pltpu.VMEM((rows, LANE), BF16)]
                   + [pltpu.VMEM((rc, n_keys), F32)] * 2
                   + [pltpu.VMEM((rc, n_keys), BF16)] * 2)

    def wrapped(*refs):
        if out_prev is not None:
            n_in = len(args)
            refs = refs[:n_in - 1] + refs[n_in:]
        kern(*refs)

    return pl.pallas_call(
        wrapped,
        grid=(n_batch, n_pair, nq),
        in_specs=in_specs,
        out_specs=pl.BlockSpec((tq, qw), lambda b, j, i: (qblk0 + b * nq + i, j)),
        out_shape=jax.ShapeDtypeStruct((t_all, out_width), BF16),
        scratch_shapes=scratch,
        input_output_aliases=aliases,
        compiler_params=_cparams(("arbitrary", "arbitrary", "arbitrary")),
        name=name,
    )(*args)


def _attn_window_kernel(sink_ref, q_ref, kl_ref, km_ref, kr_ref, vl_ref, vm_ref, vr_ref,
                        kx_ref, vx_ref, prev_ref, o_ref, *, tq, nq):
    del prev_ref
    i = pl.program_id(1)
    n_ctx = kx_ref.shape[0]
    span = 3 * tq + n_ctx
    lane = _lane_iota((tq, LANE))

    qs = []
    for t in range(3):
        qt = q_ref[:, t * LANE:(t + 1) * LANE]
        for r in range(2):
            qs.append(jnp.where((lane // HEAD_DIM) == r, qt, jnp.zeros_like(qt)))
    qs = jnp.concatenate(qs, axis=0)
    kcat = jnp.concatenate([kl_ref[...], km_ref[...], kr_ref[...], kx_ref[...]], axis=0)
    vcat = jnp.concatenate([vl_ref[...], vm_ref[...], vr_ref[...], vx_ref[...]], axis=0)
    s = lax.dot_general(qs, kcat, (((1,), (1,)), ((), ())), preferred_element_type=F32)

    r_i = lax.broadcasted_iota(jnp.int32, (tq, span), 0)
    c_i = lax.broadcasted_iota(jnp.int32, (tq, span), 1)
    never = 1 << 20
    thr_l = jnp.where(i > 0, 0, never)
    thr_r = jnp.where(i < nq - 1, 0, never)
    ok_l = (c_i >= tq) | ((c_i - r_i) >= thr_l)
    ok_r = (c_i < 2 * tq) | (c_i >= 3 * tq) | ((r_i - (c_i - 2 * tq)) >= thr_r)
    keep = ok_l & ok_r

    outs = []
    for g in range(6):
        t, r = g // 2, g % 2
        sg = jnp.where(keep, s[g * tq:(g + 1) * tq], NEG)
        sk = sink_ref[t + 3 * r] * LOG2E
        m = jnp.maximum(jnp.max(sg, axis=-1, keepdims=True), sk)
        p = jnp.exp2(sg - m)
        l = jnp.sum(p, axis=-1, keepdims=True) + jnp.exp2(sk - m)
        o = jnp.dot(p.astype(BF16), vcat, preferred_element_type=F32) / l
        outs.append(o)
    for t in range(3):
        o_ref[:, t * LANE:(t + 1) * LANE] = jnp.where(lane < HEAD_DIM, outs[2 * t],
                                                      outs[2 * t + 1]).astype(o_ref.dtype)


def _attn_window(proj, ctx_k, ctx_v, sink, out_prev, *, tok_off, n_batch, seq):
    tq = WINDOW
    nq = seq // tq
    blk0 = tok_off // tq
    n_ctx = ctx_k.shape[1]

    def kv_spec(tile, delta):
        def imap(b, i):
            return (blk0 + b * nq + jnp.clip(i + delta, 0, nq - 1), tile)
        return pl.BlockSpec((tq, LANE), imap)

    return pl.pallas_call(
        functools.partial(_attn_window_kernel, tq=tq, nq=nq),
        grid=(n_batch, nq),
        in_specs=[pl.BlockSpec(memory_space=pltpu.SMEM),
                  pl.BlockSpec((tq, 3 * LANE), lambda b, i: (blk0 + b * nq + i, QC_T // 3)),
                  kv_spec(KC_T, -1), kv_spec(KC_T, 0), kv_spec(KC_T, 1),
                  kv_spec(VC_T, -1), kv_spec(VC_T, 0), kv_spec(VC_T, 1),
                  pl.BlockSpec((None, n_ctx, LANE), lambda b, i: (b, 0, 0)),
                  pl.BlockSpec((None, n_ctx, LANE), lambda b, i: (b, 0, 0)),
                  pl.BlockSpec(memory_space=pl.ANY)],
        out_specs=pl.BlockSpec((tq, 3 * LANE), lambda b, i: (blk0 + b * nq + i, 0)),
        out_shape=jax.ShapeDtypeStruct(out_prev.shape, BF16),
        input_output_aliases={10: 0},
        compiler_params=_cparams(("arbitrary", "arbitrary")),
        name="attn_c_window",
    )(sink, proj, proj, proj, proj, proj, proj, proj, ctx_k, ctx_v, out_prev)


def _layer_norm(y, g, b):
    mu = jnp.mean(y, axis=-1, keepdims=True)
    var = jnp.mean(jnp.square(y - mu), axis=-1, keepdims=True)
    return (y - mu) * lax.rsqrt(var + LN_EPS) * g + b


def _outproj_kernel(x_ref, oa_ref, ob_ref, oc_ref, wa_ref, wb_ref, wc_ref, mod_ref, ln_ref,
                    wr_ref, br_ref, x1_ref, h2_ref, gate_ref, idx_ref, *, alpha):
    tm = x_ref.shape[0]
    o = (jnp.dot(oa_ref[...], wa_ref[...], preferred_element_type=F32)
         + jnp.dot(ob_ref[...], wb_ref[...], preferred_element_type=F32)
         + jnp.dot(oc_ref[...], wc_ref[...], preferred_element_type=F32))
    gate1 = mod_ref[2:3, :]
    shift2 = mod_ref[3:4, :]
    scale2 = mod_ref[4:5, :]
    x1 = _layer_norm(alpha * x_ref[...] + gate1 * o, ln_ref[0:1, :], ln_ref[1:2, :])
    x1_ref[...] = x1
    h2 = x1 * (1.0 + scale2) + shift2
    for j in range(TOKEN_TILE):
        h2_ref[pl.ds(j, tm, stride=TOKEN_TILE), :] = h2[:, j * LANE:(j + 1) * LANE]

    h_hi = h2.astype(BF16)
    h_lo = (h2 - h_hi.astype(F32)).astype(BF16)
    logits = (jnp.dot(h_hi, wr_ref[0], preferred_element_type=F32)
              + jnp.dot(h_hi, wr_ref[1], preferred_element_type=F32)
              + jnp.dot(h_lo, wr_ref[0], preferred_element_type=F32)
              + br_ref[0:1, :])

    lane = _lane_iota((tm, LANE))
    lane_f = lane.astype(F32)
    cur = logits
    vals, idxs = [], []
    for _ in range(TOP_K):
        mx = jnp.max(cur, axis=-1, keepdims=True)
        ix = jnp.min(jnp.where(cur == mx, lane_f, float(LANE)), axis=-1, keepdims=True)
        vals.append(mx)
        idxs.append(ix)
        cur = jnp.where(lane_f == ix, -jnp.inf, cur)
    es = [jnp.exp(v - vals[0]) for v in vals]
    den = es[0] + es[1] + es[2] + es[3]
    gates = jnp.zeros((tm, LANE), F32)
    idx = jnp.zeros((tm, LANE), F32)
    for k in range(TOP_K):
        gates = jnp.where(lane == k, es[k] / den, gates)
        idx = jnp.where(lane == k, idxs[k], idx)
    gate_ref[...] = gates
    idx_ref[...] = idx.astype(jnp.int32)


def _outproj(x, oa, ob, oc, wa, wb, wc, mod_l, ln, wr, br, *, tm, t_ctx, s_dec, alpha):
    t_all, d = x.shape
    n_ctx_tiles = t_ctx // tm
    tiles_per_seq = s_dec // tm

    def grp(i):
        return jnp.where(i < n_ctx_tiles, 0, 1 + (i - n_ctx_tiles) // tiles_per_seq)

    row = lambda w: pl.BlockSpec((tm, w), lambda i: (i, 0))
    full = lambda a: pl.BlockSpec(a.shape, lambda i: (0,) * a.ndim)
    return pl.pallas_call(
        functools.partial(_outproj_kernel, alpha=alpha),
        grid=(t_all // tm,),
        in_specs=[row(d), row(oa.shape[1]), row(ob.shape[1]), row(oc.shape[1]),
                  full(wa), full(wb), full(wc),
                  pl.BlockSpec((None, 6, d), lambda i: (grp(i), 0, 0)),
                  full(ln), full(wr), full(br)],
        out_specs=[row(d), pl.BlockSpec((tm * TOKEN_TILE, LANE), lambda i: (i, 0)), row(LANE), row(LANE)],
        out_shape=[jax.ShapeDtypeStruct((t_all, d), F32),
                   jax.ShapeDtypeStruct((t_all * TOKEN_TILE, LANE), F32),
                   jax.ShapeDtypeStruct((t_all, LANE), F32),
                   jax.ShapeDtypeStruct((t_all, LANE), jnp.int32)],
        compiler_params=_cparams(("arbitrary",)),
        name="outproj_ln_router",
    )(x, oa, ob, oc, wa, wb, wc, mod_l, ln, wr, br)


def _moe_kernel(be_ref, nused_ref, src_first, src_next, dst_prev, h_hbm, wgu_ref, bgu_ref, wdn_ref,
                bdn_ref, y_hbm, xbuf, ybuf, gsem, ssem, *, prime_row0):
    del be_ref
    i = pl.program_id(0)
    n_used = nused_ref[0]
    tt = TOKEN_TILE
    blk = xbuf.shape[1] // tt

    def tile_rows(r):
        return pl.ds(r * tt if isinstance(r, int) else pl.multiple_of(r * tt, tt), tt)

    def gather(src_ref, r, slot):
        row = pl.multiple_of(src_ref[0, r], tt)
        return pltpu.make_async_copy(h_hbm.at[pl.ds(row, tt)], xbuf.at[slot, tile_rows(r)], gsem.at[slot])

    def scatter(row, r, slot):
        row = pl.multiple_of(row, tt)
        return pltpu.make_async_copy(ybuf.at[slot, tile_rows(r)], y_hbm.at[pl.ds(row, tt)], ssem.at[slot])

    def wait_gather(slot):
        pltpu.make_async_copy(h_hbm.at[pl.ds(0, blk * tt)], xbuf.at[slot], gsem.at[slot]).wait()

    def wait_scatter(slot):
        pltpu.make_async_copy(ybuf.at[slot], y_hbm.at[pl.ds(0, blk * tt)], ssem.at[slot]).wait()

    @pl.when(i == 0)
    def _():
        ybuf[...] = jnp.zeros(ybuf.shape, F32)

        def body(r, carry):
            gather(src_first, r, 0).start()
            scatter((prime_row0 + r) * tt, r, 0).start()
            return carry
        lax.fori_loop(0, blk, body, 0)

    def compute_block(slot):
        wait_gather(slot)
        wait_scatter(slot)
        x = jnp.concatenate([xbuf[slot, pl.ds(j, blk, stride=tt), :] for j in range(tt)],
                            axis=1).astype(BF16)
        for r in range(blk):
            gather(src_next, r, 1 - slot).start()
            scatter(dst_prev[0, r], r, 1 - slot).start()
        gu = jnp.dot(x, wgu_ref[...], preferred_element_type=F32) + bgu_ref[...]
        d_ff = gu.shape[1] // 2
        glu = jnp.minimum(gu[:, :d_ff], SWIGLU_LIMIT)
        lin = jnp.clip(gu[:, d_ff:], -SWIGLU_LIMIT, SWIGLU_LIMIT)
        act = glu * jax.nn.sigmoid(SWIGLU_ALPHA * glu) * (lin + 1.0)
        y = jnp.dot(act.astype(BF16), wdn_ref[...], preferred_element_type=F32) + bdn_ref[...]
        for j in range(tt):
            ybuf[slot, pl.ds(j, blk, stride=tt), :] = y[:, j * LANE:(j + 1) * LANE]

    for parity in range(2):
        pl.when((i < n_used) & (i % 2 == parity))(functools.partial(compute_block, parity))

    @pl.when(i == n_used)
    def _():
        slot = i % 2
        wait_gather(slot)
        wait_scatter(slot)

        def body(r, carry):
            scatter(dst_prev[0, r], r, 1 - slot).start()
            return carry
        lax.fori_loop(0, blk, body, 0)
        wait_scatter(1 - slot)


def _moe(h2, block_e, row_src, row_dst, n_used, wgu, bgu, wdn, bdn, *, n_out_rows, prime_row0):
    n_blocks = block_e.shape[0]
    n_e, d, n_gu = wgu.shape
    tile_rows = MOE_BLOCK * TOKEN_TILE
    idx_spec = lambda imap: pl.BlockSpec((None, 1, MOE_BLOCK), imap, memory_space=pltpu.SMEM)
    be = lambda i, b: b[jnp.minimum(i, n_blocks - 1)]
    grid_spec = pltpu.PrefetchScalarGridSpec(
        num_scalar_prefetch=2,
        grid=(n_blocks + 1,),
        in_specs=[idx_spec(lambda i, b, nu: (0, 0, 0)),
                  idx_spec(lambda i, b, nu: (jnp.minimum(i + 1, n_blocks), 0, 0)),
                  idx_spec(lambda i, b, nu: (i, 0, 0)),
                  pl.BlockSpec(memory_space=pl.ANY),
                  pl.BlockSpec((None, d, n_gu), lambda i, b, nu: (be(i, b), 0, 0)),
                  pl.BlockSpec((None, 1, n_gu), lambda i, b, nu: (be(i, b), 0, 0)),
                  pl.BlockSpec((None, n_gu // 2, d), lambda i, b, nu: (be(i, b), 0, 0)),
                  pl.BlockSpec((None, 1, d), lambda i, b, nu: (be(i, b), 0, 0))],
        out_specs=pl.BlockSpec(memory_space=pl.ANY),
        scratch_shapes=[pltpu.VMEM((2, tile_rows, LANE), F32), pltpu.VMEM((2, tile_rows, LANE), F32),
                        pltpu.SemaphoreType.DMA((2,)), pltpu.SemaphoreType.DMA((2,))],
    )
    return pl.pallas_call(
        functools.partial(_moe_kernel, prime_row0=prime_row0),
        grid_spec=grid_spec,
        out_shape=jax.ShapeDtypeStruct((n_out_rows * TOKEN_TILE, LANE), F32),
        compiler_params=_cparams(("arbitrary",)),
        name="moe_experts",
    )(block_e, n_used, row_src, row_src, row_dst, h2, wgu, bgu.reshape(n_e, 1, n_gu), wdn,
      bdn.reshape(n_e, 1, d))


def _combine_kernel(x1_ref, gate_ref, mod_ref, ln_ref, y0_ref, y1_ref, y2_ref, y3_ref, o_ref, *, alpha):
    tm = x1_ref.shape[0]
    gates = gate_ref[...]

    def rows_of(y_ref):
        return jnp.concatenate([y_ref[pl.ds(j, tm, stride=TOKEN_TILE), :] for j in range(TOKEN_TILE)], axis=1)

    y = gates[:, 0:1] * rows_of(y0_ref)
    for k, y_ref in enumerate((y1_ref, y2_ref, y3_ref), start=1):
        y = y + gates[:, k:k + 1] * rows_of(y_ref)
    gate2 = mod_ref[5:6, :]
    o_ref[...] = _layer_norm(alpha * x1_ref[...] + gate2 * y, ln_ref[0:1, :], ln_ref[1:2, :])


def _combine(x1, gates, y4, mod_l, ln, *, tm, t_ctx, s_dec, alpha):
    t_all, d = x1.shape
    n_ctx_tiles = t_ctx // tm
    tiles_per_seq = s_dec // tm
    n_tiles = t_all // tm

    def grp(i):
        return jnp.where(i < n_ctx_tiles, 0, 1 + (i - n_ctx_tiles) // tiles_per_seq)

    slab = lambda k: pl.BlockSpec((tm * TOKEN_TILE, LANE), lambda i: (k * n_tiles + i, 0))
    return pl.pallas_call(
        functools.partial(_combine_kernel, alpha=alpha),
        grid=(n_tiles,),
        in_specs=[pl.BlockSpec((tm, d), lambda i: (i, 0)),
                  pl.BlockSpec((tm, LANE), lambda i: (i, 0)),
                  pl.BlockSpec((None, 6, d), lambda i: (grp(i), 0, 0)),
                  pl.BlockSpec(ln.shape, lambda i: (0, 0)),
                  slab(0), slab(1), slab(2), slab(3)],
        out_specs=pl.BlockSpec((tm, d), lambda i: (i, 0)),
        out_shape=jax.ShapeDtypeStruct((t_all, d), F32),
        compiler_params=_cparams(("arbitrary",)),
        name="combine_ln",
    )(x1, gates, mod_l, ln, y4, y4, y4, y4)


def _routing(idx):
    n_tok = idx.shape[0]
    n_asg = n_tok * TOP_K
    flat_e = idx.reshape(-1)
    order = jnp.argsort(flat_e).astype(jnp.int32)
    experts = jnp.arange(N_EXPERTS, dtype=jnp.int32)
    counts = jnp.sum((flat_e[:, None] == experts[None, :]).astype(jnp.int32), axis=0)
    starts = jnp.cumsum(counts) - counts
    padded = (counts + MOE_BLOCK - 1) // MOE_BLOCK * MOE_BLOCK
    pends = jnp.cumsum(padded)
    pstarts = pends - padded
    n_blocks = -(-n_asg // MOE_BLOCK) + N_EXPERTS
    n_rows = n_blocks * MOE_BLOCK

    rows = jnp.arange(n_rows, dtype=jnp.int32)
    row_e = jnp.minimum(jnp.sum((rows[:, None] >= pends[None, :]).astype(jnp.int32), axis=1), N_EXPERTS - 1)
    onehot = (row_e[:, None] == experts[None, :]).astype(jnp.int32)
    rank = rows - jnp.sum(onehot * pstarts[None, :], axis=1)
    valid = rank < jnp.sum(onehot * counts[None, :], axis=1)
    src_pos = jnp.clip(jnp.sum(onehot * starts[None, :], axis=1) + rank, 0, n_asg - 1)
    asg = order[src_pos]
    tok = asg // TOP_K
    spare0 = n_asg
    row_src = jnp.where(valid, tok, 0).astype(jnp.int32)
    row_dst = jnp.where(valid, (asg % TOP_K) * n_tok + tok, spare0 + rows).astype(jnp.int32)
    lead = spare0 + n_rows + jnp.arange(MOE_BLOCK, dtype=jnp.int32)
    row_src = jnp.concatenate([row_src, jnp.zeros((MOE_BLOCK,), jnp.int32)]).reshape(n_blocks + 1, 1, MOE_BLOCK)
    row_dst = jnp.concatenate([lead, row_dst]).reshape(n_blocks + 1, 1, MOE_BLOCK)
    row_src = row_src * TOKEN_TILE
    row_dst = row_dst * TOKEN_TILE
    block_e = row_e[::MOE_BLOCK]
    n_used = (pends[-1] // MOE_BLOCK).astype(jnp.int32).reshape(1)
    prime_row0 = spare0 + n_rows + MOE_BLOCK
    n_out_rows = prime_row0 + MOE_BLOCK
    return block_e, row_src, row_dst, n_used, n_out_rows, prime_row0


def _in_perm():
    qa, ka, va = 0, 256, 512
    qb, kb, vb = 768, 1152, 1280
    qc, kc, vc = 1408, 1792, 1920
    cols = []
    for base in (qb, qc):
        for h in GQA_ORDER:
            cols += list(range(base + h * HEAD_DIM, base + (h + 1) * HEAD_DIM))
    for base, width in ((qa, 256), (ka, 256), (kb, 128), (kc, 128), (va, 256), (vb, 128), (vc, 128)):
        cols += list(range(base, base + width))
    return jnp.asarray(cols, jnp.int32)


def _gqa_rows(base):
    rows = []
    for h in GQA_ORDER:
        rows += list(range(base + h * HEAD_DIM, base + (h + 1) * HEAD_DIM))
    return jnp.asarray(rows, jnp.int32)


def _rope_patterns(rows, tm):
    def tables(dim):
        row = jnp.repeat(jnp.arange(rows), GRID_W).astype(F32)
        col = jnp.tile(jnp.arange(GRID_W), rows).astype(F32)
        nf = dim // 4
        freqs = ROPE_THETA ** (-jnp.arange(nf, dtype=F32) / nf)
        ang = jnp.concatenate([row[:, None] * freqs, col[:, None] * freqs], -1)
        return jnp.cos(ang), jnp.sin(ang)

    out = []
    for dim in (A_QK_DIM, HEAD_DIM):
        cos, sin = tables(dim)
        nf = dim // 4
        cr, cc, sr, sc = cos[:, :nf], cos[:, nf:], sin[:, :nf], sin[:, nf:]
        cpat = jnp.concatenate([cr, cr, cc, cc], -1)
        spat = jnp.concatenate([-sr, sr, -sc, sc], -1)
        reps = LANE // dim
        out += [jnp.tile(cpat, (1, reps)), jnp.tile(spat, (1, reps))]
    tab = jnp.stack(out, 0)
    ident = jnp.stack([jnp.ones((tm, LANE), F32), jnp.zeros((tm, LANE), F32)] * 2, 0)
    return jnp.concatenate([tab, ident], axis=1)


def _ctx_cache(cache):
    b, l, p, h, d = cache.shape
    return jnp.transpose(cache, (1, 0, 2, 3, 4)).reshape(l, b, p, h * d).astype(BF16)


def kernel(x_prompt, x_sample, cache_a_k, cache_a_v, cache_b_k, cache_b_v, cache_c_k, cache_c_v, c, c_ctx, w_mod, b_mod, w_in, a_lambda, a_subln_g, b_q_norm_g, b_k_norm_g, c_sink, w_out, ln1_g, ln1_b, w_router, b_router, w_gate_up, b_gate_up, w_down, b_down, ln2_g, ln2_b):
    depth = w_in.shape[0]
    n_ctx_b, s_ctx, d = x_prompt.shape
    n_dec_b, s_dec, _ = x_sample.shape
    t_ctx = n_ctx_b * s_ctx
    t_dec = n_dec_b * s_dec
    t_all = t_ctx + t_dec
    alpha = (2 * depth) ** 0.25
    tm = 512 if (t_ctx % 512 == 0 and s_dec % 512 == 0) else 256

    x = jnp.concatenate([x_prompt.reshape(t_ctx, d), x_sample.reshape(t_dec, d)], axis=0)

    n_grp = 1 + n_dec_b
    g_pad = -(-n_grp // 8) * 8
    cond = jnp.zeros((g_pad, d), F32).at[0].set(c_ctx).at[1:n_grp].set(c)
    mod = _modulation(cond, w_mod, b_mod).reshape(depth, g_pad, 6, d)

    perm = _in_perm()
    w_in_p = jnp.take(w_in, perm, axis=2).astype(BF16)
    wa = w_out[:, 0:256].astype(BF16)
    wb = jnp.take(w_out, _gqa_rows(256), axis=1).astype(BF16)
    wc = jnp.take(w_out, _gqa_rows(640), axis=1).astype(BF16)
    wgu = w_gate_up.astype(BF16)
    wdn = w_down.astype(BF16)

    pad_e = LANE - N_EXPERTS
    wr_f = jnp.pad(w_router, ((0, 0), (0, 0), (0, pad_e)))
    wr_hi = wr_f.astype(BF16)
    wr_lo = (wr_f - wr_hi.astype(F32)).astype(BF16)
    wr = jnp.stack([wr_hi, wr_lo], axis=1)
    br = jnp.pad(b_router, ((0, 0), (0, pad_e)), constant_values=NEG).reshape(depth, 1, LANE)
    br = jnp.broadcast_to(br, (depth, 8, LANE))

    rope_tab = _rope_patterns(s_dec // GRID_W, tm)
    tile2 = lambda g: jnp.tile(g, (1, LANE // HEAD_DIM))
    gains = jnp.zeros((depth, 8, LANE), F32).at[:, 0].set(tile2(b_q_norm_g)).at[:, 1].set(tile2(b_k_norm_g))
    subln = jnp.broadcast_to(tile2(a_subln_g)[:, None, :], (depth, 8, LANE))
    ln1 = jnp.stack([ln1_g, ln1_b], axis=1)
    ln2 = jnp.stack([ln2_g, ln2_b], axis=1)

    xa_k, xa_v = _ctx_cache(cache_a_k), _ctx_cache(cache_a_v)
    xb_k, xb_v = _ctx_cache(cache_b_k), _ctx_cache(cache_b_v)
    xc_k, xc_v = _ctx_cache(cache_c_k), _ctx_cache(cache_c_v)

    tq_ctx = min(256, s_ctx)
    tq_dec = 256
    kv_layers = []
    for l in range(depth):
        lam0 = _lambda_init(l)
        mod_l = mod[l]
        proj, kv = _inproj(x, mod_l, w_in_p[l], rope_tab, gains[l], tm=tm, t_ctx=t_ctx, s_dec=s_dec)
        kv_layers.append(kv)

        common_a = dict(mode="diff", q_tile=QA_T, k_tile=KA_T, v_tile=VA_T, a_lambda=a_lambda[l],
                        subln_g=subln[l], lam_init=lam0, out_width=A_HEADS * HEAD_DIM)
        oa = _attn_full(proj, tok_off=0, n_batch=n_ctx_b, seq=s_ctx, tq=tq_ctx, name="attn_a_ctx", **common_a)
        oa = _attn_full(proj, tok_off=t_ctx, n_batch=n_dec_b, seq=s_dec, tq=tq_dec, ctx_k=xa_k[l],
                        ctx_v=xa_v[l], out_prev=oa, name="attn_a_dec", **common_a)

        common_b = dict(mode="gqa", q_tile=QB_T, k_tile=KB_T, v_tile=VB_T, out_width=B_HEADS * HEAD_DIM)
        ob = _attn_full(proj, tok_off=0, n_batch=n_ctx_b, seq=s_ctx, tq=tq_ctx // 2, name="attn_b_ctx", **common_b)
        ob = _attn_full(proj, tok_off=t_ctx, n_batch=n_dec_b, seq=s_dec, tq=tq_dec // 2, ctx_k=xb_k[l],
                        ctx_v=xb_v[l], out_prev=ob, name="attn_b_dec", **common_b)

        sink_p = c_sink[l]
        oc = _attn_full(proj, mode="gqa", tok_off=0, n_batch=n_ctx_b, seq=s_ctx, tq=tq_ctx // 2,
                        q_tile=QC_T, k_tile=KC_T, v_tile=VC_T, sink=sink_p,
                        out_width=C_HEADS * HEAD_DIM, name="attn_c_ctx")
        oc = _attn_window(proj, xc_k[l], xc_v[l], sink_p, oc, tok_off=t_ctx, n_batch=n_dec_b, seq=s_dec)

        x1, h2, gates, idx = _outproj(x, oa, ob, oc, wa[l], wb[l], wc[l], mod_l, ln1[l], wr[l], br[l],
                                      tm=tm, t_ctx=t_ctx, s_dec=s_dec, alpha=alpha)
        block_e, row_src, row_dst, n_used, n_out_rows, prime_row0 = _routing(idx[:, :TOP_K])
        y4 = _moe(h2, block_e, row_src, row_dst, n_used, wgu[l], b_gate_up[l], wdn[l], b_down[l],
                  n_out_rows=n_out_rows, prime_row0=prime_row0)
        x = _combine(x1, gates, y4, mod_l, ln2[l], tm=tm, t_ctx=t_ctx, s_dec=s_dec, alpha=alpha)

    y = x[:t_ctx].reshape(n_ctx_b, s_ctx, d)
    z = x[t_ctx:].reshape(n_dec_b, s_dec, d)
    kv_all = jnp.stack(kv_layers, axis=1).reshape(n_ctx_b, s_ctx, depth, 8 * LANE)
    kv_all = jnp.transpose(kv_all, (0, 2, 1, 3))

    def cache_out(lo, width, heads):
        return kv_all[..., lo:lo + width].reshape(n_ctx_b, depth, s_ctx, heads, width // heads)

    new_a_k = cache_out(0, 256, A_HEADS)
    new_b_k = cache_out(256, 128, B_KV_HEADS)
    new_c_k = cache_out(384, 128, C_KV_HEADS)
    new_a_v = cache_out(512, 256, A_HEADS)
    new_b_v = cache_out(768, 128, B_KV_HEADS)
    new_c_v = cache_out(896, 128, C_KV_HEADS)
    return (y, z, new_a_k, new_a_v, new_b_k, new_b_v, new_c_k, new_c_v)
```

```python
import functools
import math

import jax
import jax.numpy as jnp
from jax import lax
from jax.experimental import pallas as pl
from jax.experimental.pallas import tpu as pltpu

F32 = jnp.float32
BF16 = jnp.bfloat16

LANE = 128
TOKEN_TILE = 8
HEAD_DIM = 64
A_HEADS = 4
A_QK_DIM = 32
B_HEADS = 6
B_KV_HEADS = 2
C_HEADS = 6
C_KV_HEADS = 2
GRID_W = 64
WINDOW = 128
ROPE_THETA = 10000.0
N_EXPERTS = 32
TOP_K = 4
SWIGLU_ALPHA = 1.702
SWIGLU_LIMIT = 7.0
MOE_BLOCK = 256
LN_EPS = 1e-5
RMS_EPS = 1e-6
NEG = -1e30
LOG2E = math.log2(math.e)
VMEM_LIMIT = 56 * 1024 * 1024

QB_T, QC_T, QA_T, KA_T, KB_T, KC_T, VA_T, VB_T, VC_T = 0, 3, 6, 8, 10, 11, 12, 14, 15
GQA_ORDER = (0, 3, 1, 4, 2, 5)


def _lambda_init(layer):
    return 0.8 - 0.6 * math.exp(-0.3 * layer)


def _lane_iota(shape):
    return lax.broadcasted_iota(jnp.int32, shape, len(shape) - 1)


def _cparams(sem, vmem=VMEM_LIMIT):
    return pltpu.CompilerParams(dimension_semantics=sem, vmem_limit_bytes=vmem)


def _mod_kernel(c_ref, w_ref, b_ref, o_ref):
    c = c_ref[...]
    s = (c * jax.nn.sigmoid(c)).astype(BF16)
    o_ref[...] = jnp.dot(s, w_ref[...].astype(BF16), preferred_element_type=F32) + b_ref[...]


def _modulation(cond, w_mod, b_mod):
    depth, d, n = w_mod.shape
    g = cond.shape[0]
    tn = 1536
    return pl.pallas_call(
        _mod_kernel,
        grid=(depth, n // tn),
        in_specs=[pl.BlockSpec((g, d), lambda l, j: (0, 0)),
                  pl.BlockSpec((None, d, tn), lambda l, j: (l, 0, j)),
                  pl.BlockSpec((None, 1, tn), lambda l, j: (l, 0, j))],
        out_specs=pl.BlockSpec((None, g, tn), lambda l, j: (l, 0, j)),
        out_shape=jax.ShapeDtypeStruct((depth, g, n), F32),
        compiler_params=_cparams(("arbitrary", "arbitrary")),
        name="modulation",
    )(cond, w_mod, b_mod.reshape(depth, 1, n))


def _swap_blocks(x, blk, lane):
    up = pltpu.roll(x, LANE - blk, 1)
    dn = pltpu.roll(x, blk, 1)
    return jnp.where((lane % (2 * blk)) < blk, up, dn)


def _segment_mean_sq(x, ones_seg):
    sq = x * x
    hi = sq.astype(BF16)
    lo = (sq - hi.astype(F32)).astype(BF16)
    tot = (jnp.dot(hi, ones_seg, preferred_element_type=F32)
           + jnp.dot(lo, ones_seg, preferred_element_type=F32))
    return tot * (1.0 / HEAD_DIM)


def _inproj_kernel(x_ref, mod_ref, w_ref, rope_ref, g_ref, proj_ref, kv_ref, *, n_ctx_tiles):
    i = pl.program_id(0)
    tm = x_ref.shape[0]
    x = x_ref[...]
    shift1 = mod_ref[0:1, :]
    scale1 = mod_ref[1:2, :]
    h = (x * (1.0 + scale1) + shift1).astype(BF16)

    lane = _lane_iota((tm, LANE))
    r_i = lax.broadcasted_iota(jnp.int32, (LANE, LANE), 0)
    c_i = lax.broadcasted_iota(jnp.int32, (LANE, LANE), 1)
    ones_seg = jnp.where((r_i // HEAD_DIM) == (c_i // HEAD_DIM), 1.0, 0.0).astype(BF16)

    cos_a, sin_a, cos_b, sin_b = rope_ref[0], rope_ref[1], rope_ref[2], rope_ref[3]
    gq = g_ref[0:1, :]
    gk = g_ref[1:2, :]
    qa_scale = (A_QK_DIM ** -0.5) * LOG2E
    q_scale = (HEAD_DIM ** -0.5) * LOG2E

    def section(t0, nt):
        return jnp.dot(h, w_ref[:, t0 * LANE:(t0 + nt) * LANE], preferred_element_type=F32)

    def tile(sec, t):
        return sec[:, t * LANE:(t + 1) * LANE]

    def rope(xt, cos, sin, blk):
        return xt * cos + _swap_blocks(xt, blk, lane) * sin

    def rms(xt, g):
        return xt * lax.rsqrt(_segment_mean_sq(xt, ones_seg) + RMS_EPS) * g

    def put(t, val):
        proj_ref[:, t * LANE:(t + 1) * LANE] = val.astype(BF16)

    kv_tiles = []

    sec = section(QB_T, 3)
    for t in range(3):
        put(QB_T + t, rope(rms(tile(sec, t), gq), cos_b, sin_b, 16) * q_scale)
    sec = section(QC_T, 3)
    for t in range(3):
        put(QC_T + t, rope(tile(sec, t), cos_b, sin_b, 16) * q_scale)
    sec = section(QA_T, 4)
    for t in range(2):
        put(QA_T + t, rope(tile(sec, t), cos_a, sin_a, 8) * qa_scale)
    for t in range(2):
        ka = rope(tile(sec, 2 + t), cos_a, sin_a, 8)
        put(KA_T + t, ka)
        kv_tiles.append(ka)
    sec = section(KB_T, 2)
    kb = rope(rms(tile(sec, 0), gk), cos_b, sin_b, 16)
    put(KB_T, kb)
    kc = rope(tile(sec, 1), cos_b, sin_b, 16)
    put(KC_T, kc)
    kv_tiles += [kb, kc]
    sec = section(VA_T, 4)
    for t in range(4):
        put(VA_T + t, tile(sec, t))
        kv_tiles.append(tile(sec, t))

    @pl.when(i < n_ctx_tiles)
    def _():
        for t, val in enumerate(kv_tiles):
            kv_ref[:, t * LANE:(t + 1) * LANE] = val


def _inproj(x, mod_l, w_in_p, rope_tab, gains, *, tm, t_ctx, s_dec):
    t_all, d = x.shape
    n_ctx_tiles = t_ctx // tm
    tiles_per_seq = s_dec // tm
    s_rope_tiles = (rope_tab.shape[1] - tm) // tm

    def grp(i):
        return jnp.where(i < n_ctx_tiles, 0, 1 + (i - n_ctx_tiles) // tiles_per_seq)

    def rope_blk(i):
        return jnp.where(i < n_ctx_tiles, s_rope_tiles, (i - n_ctx_tiles) % tiles_per_seq)

    n_out = w_in_p.shape[1]
    return pl.pallas_call(
        functools.partial(_inproj_kernel, n_ctx_tiles=n_ctx_tiles),
        grid=(t_all // tm,),
        in_specs=[pl.BlockSpec((tm, d), lambda i: (i, 0)),
                  pl.BlockSpec((None, 6, d), lambda i: (grp(i), 0, 0)),
                  pl.BlockSpec((d, n_out), lambda i: (0, 0)),
                  pl.BlockSpec((4, tm, LANE), lambda i: (0, rope_blk(i), 0)),
                  pl.BlockSpec((8, LANE), lambda i: (0, 0))],
        out_specs=[pl.BlockSpec((tm, n_out), lambda i: (i, 0)),
                   pl.BlockSpec((tm, 8 * LANE), lambda i: (jnp.minimum(i, n_ctx_tiles - 1), 0))],
        out_shape=[jax.ShapeDtypeStruct((t_all, n_out), BF16),
                   jax.ShapeDtypeStruct((t_ctx, 8 * LANE), F32)],
        compiler_params=_cparams(("arbitrary",)),
        name="inproj",
    )(x, mod_l, w_in_p, rope_tab, gains)


def _attn_full_kernel(*refs, mode, nt, tq, n_new, n_ctx, has_sink, lam_init):
    refs = list(refs)
    sink_ref = refs.pop(0) if has_sink else None
    q_ref, k_ref, v_ref = refs[:3]
    refs = refs[3:]
    if n_ctx:
        kx_ref, vx_ref = refs[:2]
        refs = refs[2:]
    if mode == "diff":
        alam_ref, g_ref = refs[:2]
        refs = refs[2:]
        o_ref, qs_ref, s0, s1, f0, f1, p0, p1 = refs
        f_bufs = (f0, f1)
    else:
        o_ref, qs_ref, s0, s1, p0, p1 = refs
    s_bufs, p_bufs = (s0, s1), (p0, p1)

    n_grp = 4 if mode == "diff" else 2
    gw = LANE // n_grp
    rows = nt * n_grp * tq
    lane = _lane_iota((tq, LANE))

    rb = min(128, tq)
    rc = min(256, rows)
    nb = tq // rb

    if mode == "diff":
        lane_b = _lane_iota((rb, LANE))
        for hh in range(2):
            for b in range(nb):
                for mp in range(2):
                    row0 = ((hh * nb + b) * 2 + mp) * rb
                    qs_ref[row0:row0 + rb, :] = jnp.where(
                        (lane_b // gw) == 2 * hh + mp, q_ref[b * rb:(b + 1) * rb, :],
                        jnp.zeros((rb, LANE), BF16))
    else:
        for t in range(nt):
            qt = q_ref[:, t * LANE:(t + 1) * LANE]
            for r in range(n_grp):
                row0 = (t * n_grp + r) * tq
                qs_ref[row0:row0 + tq, :] = jnp.where((lane // gw) == r, qt, jnp.zeros_like(qt))

    kt = min(512, n_new)
    tiles = [(k_ref, c * kt) for c in range(n_new // kt)]
    if n_ctx:
        tiles += [(kx_ref, c * kt) for c in range(n_ctx // kt)]

    def at(off, n):
        return slice(off, off + n)

    def tile_cols(c):
        return slice(c * LANE, (c + 1) * LANE)

    def scores(s_buf, q0, n):
        for c, (ref, off) in enumerate(tiles):
            s_buf[at(0, n), c * kt:(c + 1) * kt] = lax.dot_general(
                qs_ref[q0:q0 + n, :], ref[off:off + kt, :], (((1,), (1,)), ((), ())),
                preferred_element_type=F32)

    nl = (n_new + n_ctx) // LANE

    def row_max(s_buf, off):
        mp = s_buf[at(off, rb), tile_cols(0)]
        for c in range(1, nl):
            mp = jnp.maximum(mp, s_buf[at(off, rb), tile_cols(c)])
        return jnp.max(mp, axis=-1, keepdims=True)

    def exp_pass(s_buf, off, m, keep):
        mb = jnp.broadcast_to(m, (rb, LANE))
        lp = None
        for c in range(nl):
            p = jnp.exp2(s_buf[at(off, rb), tile_cols(c)] - mb)
            lp = p if lp is None else lp + p
            keep(c, p)
        return jnp.sum(lp, axis=-1, keepdims=True)

    def values(p_buf, n):
        acc = jnp.dot(p_buf[at(0, n), :n_new], v_ref[...], preferred_element_type=F32)
        if n_ctx:
            acc = acc + jnp.dot(p_buf[at(0, n), n_new:], vx_ref[...], preferred_element_type=F32)
        return acc

    if mode == "diff":
        a = alam_ref[...]
        lam = (jnp.exp(jnp.sum(a[0:1, :] * a[1:2, :], axis=-1, keepdims=True))
               - jnp.exp(jnp.sum(a[2:3, :] * a[3:4, :], axis=-1, keepdims=True)) + lam_init)
        heads = []
        for hh in range(2):
            for b in range(nb):
                k = hh * nb + b
                s_buf, f_buf, p_buf = s_bufs[k % 2], f_bufs[k % 2], p_bufs[hh]
                scores(s_buf, k * 2 * rb, 2 * rb)
                ls = []
                for off in (0, rb):
                    def keep_f32(c, p, off=off, f_buf=f_buf):
                        f_buf[at(off, rb), tile_cols(c)] = p
                    ls.append(exp_pass(s_buf, off, row_max(s_buf, off), keep_f32))
                w1 = jnp.broadcast_to(1.0 / ls[0], (rb, LANE))
                w2 = jnp.broadcast_to(lam / ls[1], (rb, LANE))
                for c in range(nl):
                    pc = f_buf[at(0, rb), tile_cols(c)] * w1 - f_buf[at(rb, rb), tile_cols(c)] * w2
                    p_buf[at(b * rb, rb), tile_cols(c)] = pc.astype(BF16)
            heads.append(values(p_bufs[hh], tq))
        out = jnp.where(lane < HEAD_DIM, heads[0], heads[1])
        sq = out * out
        ms_lo = jnp.sum(jnp.where(lane < HEAD_DIM, sq, 0.0), axis=-1, keepdims=True)
        ms_hi = jnp.sum(jnp.where(lane < HEAD_DIM, 0.0, sq), axis=-1, keepdims=True)
        ms = jnp.where(lane < HEAD_DIM, ms_lo, ms_hi) * (1.0 / HEAD_DIM)
        out = out * lax.rsqrt(ms + RMS_EPS) * g_ref[0:1, :] * (1.0 - lam_init)
        o_ref[...] = out.astype(o_ref.dtype)
        return

    accs, ls = [], []
    for ci, c0 in enumerate(range(0, rows, rc)):
        s_buf, p_buf = s_bufs[ci % 2], p_bufs[ci % 2]
        scores(s_buf, c0, rc)
        for off in range(0, rc, rb):
            m = row_max(s_buf, off)
            if has_sink:
                g = (c0 + off) // tq
                sk = sink_ref[(g // n_grp) + 3 * (g % n_grp)] * LOG2E
                m = jnp.maximum(m, sk)

            def keep_bf16(c, p, off=off, p_buf=p_buf):
                p_buf[at(off, rb), tile_cols(c)] = p.astype(BF16)
            l = exp_pass(s_buf, off, m, keep_bf16)
            if has_sink:
                l = l + jnp.exp2(sk - m)
            ls.append(l)
        accs.append(values(p_buf, rc))
    o = jnp.concatenate(accs, axis=0) / jnp.concatenate(ls, axis=0)
    for t in range(nt):
        lo = o[(2 * t) * tq:(2 * t + 1) * tq]
        hi = o[(2 * t + 1) * tq:(2 * t + 2) * tq]
        o_ref[:, t * LANE:(t + 1) * LANE] = jnp.where(lane < HEAD_DIM, lo, hi).astype(o_ref.dtype)


def _attn_full(proj, *, mode, tok_off, n_batch, seq, tq, q_tile, k_tile, v_tile,
               ctx_k=None, ctx_v=None, ctx_tile=0, sink=None, a_lambda=None, subln_g=None,
               lam_init=0.0, out_prev=None, out_width=None, name="attn"):
    t_all = proj.shape[0]
    nt = 1 if mode == "diff" else 3
    n_pair = 2 if mode == "diff" else 1
    n_grp = 4 if mode == "diff" else 2
    rows = nt * n_grp * tq
    nq = seq // tq
    qblk0 = tok_off // tq
    sblk0 = tok_off // seq
    n_ctx = 0 if ctx_k is None else ctx_k.shape[1]
    n_keys = seq + n_ctx
    qw = nt * LANE
    qcol0 = q_tile // nt

    args, in_specs = [], []
    if sink is not None:
        args.append(sink)
        in_specs.append(pl.BlockSpec(memory_space=pltpu.SMEM))
    args += [proj, proj, proj]
    in_specs += [
        pl.BlockSpec((tq, qw), lambda b, j, i: (qblk0 + b * nq + i, qcol0 + j)),
        pl.BlockSpec((seq, LANE), lambda b, j, i: (sblk0 + b, k_tile + j)),
        pl.BlockSpec((seq, LANE), lambda b, j, i: (sblk0 + b, v_tile + j)),
    ]
    if n_ctx:
        args += [ctx_k, ctx_v]
        in_specs += [pl.BlockSpec((None, n_ctx, LANE), lambda b, j, i: (b, 0, ctx_tile + j)),
                     pl.BlockSpec((None, n_ctx, LANE), lambda b, j, i: (b, 0, ctx_tile + j))]
    if mode == "diff":
        args += [a_lambda, subln_g]
        in_specs += [pl.BlockSpec(a_lambda.shape, lambda b, j, i: (0, 0)),
                     pl.BlockSpec(subln_g.shape, lambda b, j, i: (0, 0))]
    aliases = {}
    if out_prev is not None:
        aliases = {len(args): 0}
        args.append(out_prev)
        in_specs.append(pl.BlockSpec(memory_space=pl.ANY))

    kern = functools.partial(_attn_full_kernel, mode=mode, nt=nt, tq=tq, n_new=seq,
                             n_ctx=n_ctx, has_sink=sink is not None, lam_init=lam_init)

    rb = min(128, tq)
    if mode == "diff":
        scratch = ([pltpu.VMEM((rows, LANE), BF16)]
                   + [pltpu.VMEM((2 * rb, n_keys), F32)] * 4
                   + [pltpu.VMEM((tq, n_keys), BF16)] * 2)
    else:
        rc = min(256, rows)
        scratch = ([pltpu.VMEM((rows, LANE), BF16)]
                   + [pltpu.VMEM((rc, n_keys), F32)] * 2
                   + [pltpu.VMEM((rc, n_keys), BF16)] * 2)

    def wrapped(*refs):
        if out_prev is not None:
            n_in = len(args)
            refs = refs[:n_in - 1] + refs[n_in:]
        kern(*refs)

    return pl.pallas_call(
        wrapped,
        grid=(n_batch, n_pair, nq),
        in_specs=in_specs,
        out_specs=pl.BlockSpec((tq, qw), lambda b, j, i: (qblk0 + b * nq + i, j)),
        out_shape=jax.ShapeDtypeStruct((t_all, out_width), BF16),
        scratch_shapes=scratch,
        input_output_aliases=aliases,
        compiler_params=_cparams(("arbitrary", "arbitrary", "arbitrary")),
        name=name,
    )(*args)


def _attn_window_kernel(sink_ref, q_ref, kl_ref, km_ref, kr_ref, vl_ref, vm_ref, vr_ref,
                        kx_ref, vx_ref, prev_ref, o_ref, *, tq, nq):
    del prev_ref
    i = pl.program_id(1)
    n_ctx = kx_ref.shape[0]
    span = 3 * tq + n_ctx
    lane = _lane_iota((tq, LANE))

    qs = []
    for t in range(3):
        qt = q_ref[:, t * LANE:(t + 1) * LANE]
        for r in range(2):
            qs.append(jnp.where((lane // HEAD_DIM) == r, qt, jnp.zeros_like(qt)))
    qs = jnp.concatenate(qs, axis=0)
    kcat = jnp.concatenate([kl_ref[...], km_ref[...], kr_ref[...], kx_ref[...]], axis=0)
    vcat = jnp.concatenate([vl_ref[...], vm_ref[...], vr_ref[...], vx_ref[...]], axis=0)
    s = lax.dot_general(qs, kcat, (((1,), (1,)), ((), ())), preferred_element_type=F32)

    r_i = lax.broadcasted_iota(jnp.int32, (tq, span), 0)
    c_i = lax.broadcasted_iota(jnp.int32, (tq, span), 1)
    never = 1 << 20
    thr_l = jnp.where(i > 0, 0, never)
    thr_r = jnp.where(i < nq - 1, 0, never)
    ok_l = (c_i >= tq) | ((c_i - r_i) >= thr_l)
    ok_r = (c_i < 2 * tq) | (c_i >= 3 * tq) | ((r_i - (c_i - 2 * tq)) >= thr_r)
    keep = ok_l & ok_r

    outs = []
    for g in range(6):
        t, r = g // 2, g % 2
        sg = jnp.where(keep, s[g * tq:(g + 1) * tq], NEG)
        sk = sink_ref[t + 3 * r] * LOG2E
        m = jnp.maximum(jnp.max(sg, axis=-1, keepdims=True), sk)
        p = jnp.exp2(sg - m)
        l = jnp.sum(p, axis=-1, keepdims=True) + jnp.exp2(sk - m)
        o = jnp.dot(p.astype(BF16), vcat, preferred_element_type=F32) / l
        outs.append(o)
    for t in range(3):
        o_ref[:, t * LANE:(t + 1) * LANE] = jnp.where(lane < HEAD_DIM, outs[2 * t],
                                                      outs[2 * t + 1]).astype(o_ref.dtype)


def _attn_window(proj, ctx_k, ctx_v, sink, out_prev, *, tok_off, n_batch, seq):
    tq = WINDOW
    nq = seq // tq
    blk0 = tok_off // tq
    n_ctx = ctx_k.shape[1]

    def kv_spec(tile, delta):
        def imap(b, i):
            return (blk0 + b * nq + jnp.clip(i + delta, 0, nq - 1), tile)
        return pl.BlockSpec((tq, LANE), imap)

    return pl.pallas_call(
        functools.partial(_attn_window_kernel, tq=tq, nq=nq),
        grid=(n_batch, nq),
        in_specs=[pl.BlockSpec(memory_space=pltpu.SMEM),
                  pl.BlockSpec((tq, 3 * LANE), lambda b, i: (blk0 + b * nq + i, QC_T // 3)),
                  kv_spec(KC_T, -1), kv_spec(KC_T, 0), kv_spec(KC_T, 1),
                  kv_spec(VC_T, -1), kv_spec(VC_T, 0), kv_spec(VC_T, 1),
                  pl.BlockSpec((None, n_ctx, LANE), lambda b, i: (b, 0, 0)),
                  pl.BlockSpec((None, n_ctx, LANE), lambda b, i: (b, 0, 0)),
                  pl.BlockSpec(memory_space=pl.ANY)],
        out_specs=pl.BlockSpec((tq, 3 * LANE), lambda b, i: (blk0 + b * nq + i, 0)),
        out_shape=jax.ShapeDtypeStruct(out_prev.shape, BF16),
        input_output_aliases={10: 0},
        compiler_params=_cparams(("arbitrary", "arbitrary")),
        name="attn_c_window",
    )(sink, proj, proj, proj, proj, proj, proj, proj, ctx_k, ctx_v, out_prev)


def _layer_norm(y, g, b):
    mu = jnp.mean(y, axis=-1, keepdims=True)
    var = jnp.mean(jnp.square(y - mu), axis=-1, keepdims=True)
    return (y - mu) * lax.rsqrt(var + LN_EPS) * g + b


def _outproj_kernel(x_ref, oa_ref, ob_ref, oc_ref, wa_ref, wb_ref, wc_ref, mod_ref, ln_ref,
                    wr_ref, br_ref, x1_ref, h2_ref, gate_ref, idx_ref, *, alpha):
    tm = x_ref.shape[0]
    o = (jnp.dot(oa_ref[...], wa_ref[...], preferred_element_type=F32)
         + jnp.dot(ob_ref[...], wb_ref[...], preferred_element_type=F32)
         + jnp.dot(oc_ref[...], wc_ref[...], preferred_element_type=F32))
    gate1 = mod_ref[2:3, :]
    shift2 = mod_ref[3:4, :]
    scale2 = mod_ref[4:5, :]
    x1 = _layer_norm(alpha * x_ref[...] + gate1 * o, ln_ref[0:1, :], ln_ref[1:2, :])
    x1_ref[...] = x1
    h2 = x1 * (1.0 + scale2) + shift2
    for j in range(TOKEN_TILE):
        h2_ref[pl.ds(j, tm, stride=TOKEN_TILE), :] = h2[:, j * LANE:(j + 1) * LANE]

    h_hi = h2.astype(BF16)
    h_lo = (h2 - h_hi.astype(F32)).astype(BF16)
    logits = (jnp.dot(h_hi, wr_ref[0], preferred_element_type=F32)
              + jnp.dot(h_hi, wr_ref[1], preferred_element_type=F32)
              + jnp.dot(h_lo, wr_ref[0], preferred_element_type=F32)
              + br_ref[0:1, :])

    lane = _lane_iota((tm, LANE))
    lane_f = lane.astype(F32)
    cur = logits
    vals, idxs = [], []
    for _ in range(TOP_K):
        mx = jnp.max(cur, axis=-1, keepdims=True)
        ix = jnp.min(jnp.where(cur == mx, lane_f, float(LANE)), axis=-1, keepdims=True)
        vals.append(mx)
        idxs.append(ix)
        cur = jnp.where(lane_f == ix, -jnp.inf, cur)
    es = [jnp.exp(v - vals[0]) for v in vals]
    den = es[0] + es[1] + es[2] + es[3]
    gates = jnp.zeros((tm, LANE), F32)
    idx = jnp.zeros((tm, LANE), F32)
    for k in range(TOP_K):
        gates = jnp.where(lane == k, es[k] / den, gates)
        idx = jnp.where(lane == k, idxs[k], idx)
    gate_ref[...] = gates
    idx_ref[...] = idx.astype(jnp.int32)


def _outproj(x, oa, ob, oc, wa, wb, wc, mod_l, ln, wr, br, *, tm, t_ctx, s_dec, alpha):
    t_all, d = x.shape
    n_ctx_tiles = t_ctx // tm
    tiles_per_seq = s_dec // tm

    def grp(i):
        return jnp.where(i < n_ctx_tiles, 0, 1 + (i - n_ctx_tiles) // tiles_per_seq)

    row = lambda w: pl.BlockSpec((tm, w), lambda i: (i, 0))
    full = lambda a: pl.BlockSpec(a.shape, lambda i: (0,) * a.ndim)
    return pl.pallas_call(
        functools.partial(_outproj_kernel, alpha=alpha),
        grid=(t_all // tm,),
        in_specs=[row(d), row(oa.shape[1]), row(ob.shape[1]), row(oc.shape[1]),
                  full(wa), full(wb), full(wc),
                  pl.BlockSpec((None, 6, d), lambda i: (grp(i), 0, 0)),
                  full(ln), full(wr), full(br)],
        out_specs=[row(d), pl.BlockSpec((tm * TOKEN_TILE, LANE), lambda i: (i, 0)), row(LANE), row(LANE)],
        out_shape=[jax.ShapeDtypeStruct((t_all, d), F32),
                   jax.ShapeDtypeStruct((t_all * TOKEN_TILE, LANE), F32),
                   jax.ShapeDtypeStruct((t_all, LANE), F32),
                   jax.ShapeDtypeStruct((t_all, LANE), jnp.int32)],
        compiler_params=_cparams(("arbitrary",)),
        name="outproj_ln_router",
    )(x, oa, ob, oc, wa, wb, wc, mod_l, ln, wr, br)


def _moe_kernel(be_ref, nused_ref, src_first, src_next, dst_prev, h_hbm, wgu_ref, bgu_ref, wdn_ref,
                bdn_ref, y_hbm, xbuf, ybuf, wgu_bf, wdn_bf, gsem, ssem, *, prime_row0, n_blocks):
    i = pl.program_id(0)
    n_used = nused_ref[0]

    e_cur = be_ref[jnp.minimum(i, n_blocks - 1)]
    e_prev = be_ref[jnp.maximum(i - 1, 0)]

    @pl.when((i < n_used) & ((i == 0) | (e_cur != e_prev)))
    def _():
        wgu_bf[...] = wgu_ref[...].astype(BF16)
        wdn_bf[...] = wdn_ref[...].astype(BF16)

    tt = TOKEN_TILE
    blk = xbuf.shape[1] // tt

    def tile_rows(r):
        return pl.ds(r * tt if isinstance(r, int) else pl.multiple_of(r * tt, tt), tt)

    def gather(src_ref, r, slot):
        row = pl.multiple_of(src_ref[0, r], tt)
        return pltpu.make_async_copy(h_hbm.at[pl.ds(row, tt)], xbuf.at[slot, tile_rows(r)], gsem.at[slot])

    def scatter(row, r, slot):
        row = pl.multiple_of(row, tt)
        return pltpu.make_async_copy(ybuf.at[slot, tile_rows(r)], y_hbm.at[pl.ds(row, tt)], ssem.at[slot])

    def wait_gather(slot):
        pltpu.make_async_copy(h_hbm.at[pl.ds(0, blk * tt)], xbuf.at[slot], gsem.at[slot]).wait()

    def wait_scatter(slot):
        pltpu.make_async_copy(ybuf.at[slot], y_hbm.at[pl.ds(0, blk * tt)], ssem.at[slot]).wait()

    @pl.when(i == 0)
    def _():
        ybuf[...] = jnp.zeros(ybuf.shape, F32)

        def body(r, carry):
            gather(src_first, r, 0).start()
            scatter((prime_row0 + r) * tt, r, 0).start()
            return carry
        lax.fori_loop(0, blk, body, 0)

    def compute_block(slot):
        wait_gather(slot)
        wait_scatter(slot)
        x = jnp.concatenate([xbuf[slot, pl.ds(j, blk, stride=tt), :] for j in range(tt)],
                            axis=1).astype(BF16)
        for r in range(blk):
            gather(src_next, r, 1 - slot).start(priority=r % 2)
            scatter(dst_prev[0, r], r, 1 - slot).start(priority=(r + 1) % 2)
        gu = jnp.dot(x, wgu_bf[...], preferred_element_type=F32) + bgu_ref[...]
        d_ff = gu.shape[1] // 2
        glu = jnp.minimum(gu[:, :d_ff], SWIGLU_LIMIT)
        lin = jnp.clip(gu[:, d_ff:], -SWIGLU_LIMIT, SWIGLU_LIMIT)
        act = glu * jax.nn.sigmoid(SWIGLU_ALPHA * glu) * (lin + 1.0)
        y = jnp.dot(act.astype(BF16), wdn_bf[...], preferred_element_type=F32) + bdn_ref[...]
        for j in range(tt):
            ybuf[slot, pl.ds(j, blk, stride=tt), :] = y[:, j * LANE:(j + 1) * LANE]

    for parity in range(2):
        pl.when((i < n_used) & (i % 2 == parity))(functools.partial(compute_block, parity))

    @pl.when(i == n_used)
    def _():
        slot = i % 2
        wait_gather(slot)
        wait_scatter(slot)

        def body(r, carry):
            scatter(dst_prev[0, r], r, 1 - slot).start()
            return carry
        lax.fori_loop(0, blk, body, 0)
        wait_scatter(1 - slot)


def _moe(h2, block_e, row_src, row_dst, n_used, wgu, bgu, wdn, bdn, *, layer, n_out_rows, prime_row0):
    n_blocks = block_e.shape[0]
    depth, n_e, d, n_gu = wgu.shape
    tile_rows = MOE_BLOCK * TOKEN_TILE
    idx_spec = lambda imap: pl.BlockSpec((None, 1, MOE_BLOCK), imap, memory_space=pltpu.SMEM)
    be = lambda i, b: b[jnp.minimum(i, n_blocks - 1)]
    grid_spec = pltpu.PrefetchScalarGridSpec(
        num_scalar_prefetch=2,
        grid=(n_blocks + 1,),
        in_specs=[idx_spec(lambda i, b, nu: (0, 0, 0)),
                  idx_spec(lambda i, b, nu: (jnp.minimum(i + 1, n_blocks), 0, 0)),
                  idx_spec(lambda i, b, nu: (i, 0, 0)),
                  pl.BlockSpec(memory_space=pl.ANY),
                  pl.BlockSpec((None, None, d, n_gu), lambda i, b, nu: (layer, be(i, b), 0, 0)),
                  pl.BlockSpec((None, None, 1, n_gu), lambda i, b, nu: (layer, be(i, b), 0, 0)),
                  pl.BlockSpec((None, None, n_gu // 2, d), lambda i, b, nu: (layer, be(i, b), 0, 0)),
                  pl.BlockSpec((None, None, 1, d), lambda i, b, nu: (layer, be(i, b), 0, 0))],
        out_specs=pl.BlockSpec(memory_space=pl.ANY),
        scratch_shapes=[pltpu.VMEM((2, tile_rows, LANE), F32), pltpu.VMEM((2, tile_rows, LANE), F32),
                        pltpu.VMEM((d, n_gu), BF16), pltpu.VMEM((n_gu // 2, d), BF16),
                        pltpu.SemaphoreType.DMA((2,)), pltpu.SemaphoreType.DMA((2,))],
    )
    return pl.pallas_call(
        functools.partial(_moe_kernel, prime_row0=prime_row0, n_blocks=n_blocks),
        grid_spec=grid_spec,
        out_shape=jax.ShapeDtypeStruct((n_out_rows * TOKEN_TILE, LANE), F32),
        compiler_params=_cparams(("arbitrary",)),
        name="moe_experts",
    )(block_e, n_used, row_src, row_src, row_dst, h2, wgu, bgu.reshape(depth, n_e, 1, n_gu), wdn,
      bdn.reshape(depth, n_e, 1, d))


def _combine_kernel(x1_ref, gate_ref, mod_ref, ln_ref, y0_ref, y1_ref, y2_ref, y3_ref, o_ref, *, alpha):
    tm = x1_ref.shape[0]
    gates = gate_ref[...]

    def rows_of(y_ref):
        return jnp.concatenate([y_ref[pl.ds(j, tm, stride=TOKEN_TILE), :] for j in range(TOKEN_TILE)], axis=1)

    y = gates[:, 0:1] * rows_of(y0_ref)
    for k, y_ref in enumerate((y1_ref, y2_ref, y3_ref), start=1):
        y = y + gates[:, k:k + 1] * rows_of(y_ref)
    gate2 = mod_ref[5:6, :]
    o_ref[...] = _layer_norm(alpha * x1_ref[...] + gate2 * y, ln_ref[0:1, :], ln_ref[1:2, :])


def _combine(x1, gates, y4, mod_l, ln, *, tm, t_ctx, s_dec, alpha):
    t_all, d = x1.shape
    n_ctx_tiles = t_ctx // tm
    tiles_per_seq = s_dec // tm
    n_tiles = t_all // tm

    def grp(i):
        return jnp.where(i < n_ctx_tiles, 0, 1 + (i - n_ctx_tiles) // tiles_per_seq)

    slab = lambda k: pl.BlockSpec((tm * TOKEN_TILE, LANE), lambda i: (k * n_tiles + i, 0))
    return pl.pallas_call(
        functools.partial(_combine_kernel, alpha=alpha),
        grid=(n_tiles,),
        in_specs=[pl.BlockSpec((tm, d), lambda i: (i, 0)),
                  pl.BlockSpec((tm, LANE), lambda i: (i, 0)),
                  pl.BlockSpec((None, 6, d), lambda i: (grp(i), 0, 0)),
                  pl.BlockSpec(ln.shape, lambda i: (0, 0)),
                  slab(0), slab(1), slab(2), slab(3)],
        out_specs=pl.BlockSpec((tm, d), lambda i: (i, 0)),
        out_shape=jax.ShapeDtypeStruct((t_all, d), F32),
        compiler_params=_cparams(("arbitrary",)),
        name="combine_ln",
    )(x1, gates, mod_l, ln, y4, y4, y4, y4)


def _routing(idx):
    n_tok = idx.shape[0]
    n_asg = n_tok * TOP_K
    flat_e = idx.reshape(-1)
    order = jnp.argsort(flat_e).astype(jnp.int32)
    experts = jnp.arange(N_EXPERTS, dtype=jnp.int32)
    counts = jnp.sum((flat_e[:, None] == experts[None, :]).astype(jnp.int32), axis=0)
    starts = jnp.cumsum(counts) - counts
    padded = (counts + MOE_BLOCK - 1) // MOE_BLOCK * MOE_BLOCK
    pends = jnp.cumsum(padded)
    pstarts = pends - padded
    n_blocks = -(-n_asg // MOE_BLOCK) + N_EXPERTS
    n_rows = n_blocks * MOE_BLOCK

    rows = jnp.arange(n_rows, dtype=jnp.int32)
    row_e = jnp.minimum(jnp.sum((rows[:, None] >= pends[None, :]).astype(jnp.int32), axis=1), N_EXPERTS - 1)
    onehot = (row_e[:, None] == experts[None, :]).astype(jnp.int32)
    rank = rows - jnp.sum(onehot * pstarts[None, :], axis=1)
    valid = rank < jnp.sum(onehot * counts[None, :], axis=1)
    src_pos = jnp.clip(jnp.sum(onehot * starts[None, :], axis=1) + rank, 0, n_asg - 1)
    asg = order[src_pos]
    tok = asg // TOP_K
    spare0 = n_asg
    row_src = jnp.where(valid, tok, 0).astype(jnp.int32)
    row_dst = jnp.where(valid, (asg % TOP_K) * n_tok + tok, spare0 + rows).astype(jnp.int32)
    lead = spare0 + n_rows + jnp.arange(MOE_BLOCK, dtype=jnp.int32)
    row_src = jnp.concatenate([row_src, jnp.zeros((MOE_BLOCK,), jnp.int32)]).reshape(n_blocks + 1, 1, MOE_BLOCK)
    row_dst = jnp.concatenate([lead, row_dst]).reshape(n_blocks + 1, 1, MOE_BLOCK)
    row_src = row_src * TOKEN_TILE
    row_dst = row_dst * TOKEN_TILE
    block_e = row_e[::MOE_BLOCK]
    n_used = (pends[-1] // MOE_BLOCK).astype(jnp.int32).reshape(1)
    prime_row0 = spare0 + n_rows + MOE_BLOCK
    n_out_rows = prime_row0 + MOE_BLOCK
    return block_e, row_src, row_dst, n_used, n_out_rows, prime_row0


def _in_perm():
    qa, ka, va = 0, 256, 512
    qb, kb, vb = 768, 1152, 1280
    qc, kc, vc = 1408, 1792, 1920
    cols = []
    for base in (qb, qc):
        for h in GQA_ORDER:
            cols += list(range(base + h * HEAD_DIM, base + (h + 1) * HEAD_DIM))
    for base, width in ((qa, 256), (ka, 256), (kb, 128), (kc, 128), (va, 256), (vb, 128), (vc, 128)):
        cols += list(range(base, base + width))
    return jnp.asarray(cols, jnp.int32)


def _gqa_rows(base):
    rows = []
    for h in GQA_ORDER:
        rows += list(range(base + h * HEAD_DIM, base + (h + 1) * HEAD_DIM))
    return jnp.asarray(rows, jnp.int32)


def _rope_patterns(rows, tm):
    def tables(dim):
        row = jnp.repeat(jnp.arange(rows), GRID_W).astype(F32)
        col = jnp.tile(jnp.arange(GRID_W), rows).astype(F32)
        nf = dim // 4
        freqs = ROPE_THETA ** (-jnp.arange(nf, dtype=F32) / nf)
        ang = jnp.concatenate([row[:, None] * freqs, col[:, None] * freqs], -1)
        return jnp.cos(ang), jnp.sin(ang)

    out = []
    for dim in (A_QK_DIM, HEAD_DIM):
        cos, sin = tables(dim)
        nf = dim // 4
        cr, cc, sr, sc = cos[:, :nf], cos[:, nf:], sin[:, :nf], sin[:, nf:]
        cpat = jnp.concatenate([cr, cr, cc, cc], -1)
        spat = jnp.concatenate([-sr, sr, -sc, sc], -1)
        reps = LANE // dim
        out += [jnp.tile(cpat, (1, reps)), jnp.tile(spat, (1, reps))]
    tab = jnp.stack(out, 0)
    ident = jnp.stack([jnp.ones((tm, LANE), F32), jnp.zeros((tm, LANE), F32)] * 2, 0)
    return jnp.concatenate([tab, ident], axis=1)


def _ctx_cache(cache):
    b, l, p, h, d = cache.shape
    return jnp.transpose(cache, (1, 0, 2, 3, 4)).reshape(l, b, p, h * d).astype(BF16)


def kernel(x_prompt, x_sample, cache_a_k, cache_a_v, cache_b_k, cache_b_v, cache_c_k, cache_c_v, c, c_ctx, w_mod, b_mod, w_in, a_lambda, a_subln_g, b_q_norm_g, b_k_norm_g, c_sink, w_out, ln1_g, ln1_b, w_router, b_router, w_gate_up, b_gate_up, w_down, b_down, ln2_g, ln2_b):
    depth = w_in.shape[0]
    n_ctx_b, s_ctx, d = x_prompt.shape
    n_dec_b, s_dec, _ = x_sample.shape
    t_ctx = n_ctx_b * s_ctx
    t_dec = n_dec_b * s_dec
    t_all = t_ctx + t_dec
    alpha = (2 * depth) ** 0.25
    tm = 512 if (t_ctx % 512 == 0 and s_dec % 512 == 0) else 256

    x = jnp.concatenate([x_prompt.reshape(t_ctx, d), x_sample.reshape(t_dec, d)], axis=0)

    n_grp = 1 + n_dec_b
    g_pad = -(-n_grp // 8) * 8
    cond = jnp.zeros((g_pad, d), F32).at[0].set(c_ctx).at[1:n_grp].set(c)
    mod = _modulation(cond, w_mod, b_mod).reshape(depth, g_pad, 6, d)

    perm = _in_perm()
    w_in_p = jnp.take(w_in, perm, axis=2).astype(BF16)
    wa = w_out[:, 0:256].astype(BF16)
    wb = jnp.take(w_out, _gqa_rows(256), axis=1).astype(BF16)
    wc = jnp.take(w_out, _gqa_rows(640), axis=1).astype(BF16)

    pad_e = LANE - N_EXPERTS
    wr_f = jnp.pad(w_router, ((0, 0), (0, 0), (0, pad_e)))
    wr_hi = wr_f.astype(BF16)
    wr_lo = (wr_f - wr_hi.astype(F32)).astype(BF16)
    wr = jnp.stack([wr_hi, wr_lo], axis=1)
    br = jnp.pad(b_router, ((0, 0), (0, pad_e)), constant_values=NEG).reshape(depth, 1, LANE)
    br = jnp.broadcast_to(br, (depth, 8, LANE))

    rope_tab = _rope_patterns(s_dec // GRID_W, tm)
    tile2 = lambda g: jnp.tile(g, (1, LANE // HEAD_DIM))
    gains = jnp.zeros((depth, 8, LANE), F32).at[:, 0].set(tile2(b_q_norm_g)).at[:, 1].set(tile2(b_k_norm_g))
    subln = jnp.broadcast_to(tile2(a_subln_g)[:, None, :], (depth, 8, LANE))
    ln1 = jnp.stack([ln1_g, ln1_b], axis=1)
    ln2 = jnp.stack([ln2_g, ln2_b], axis=1)

    xa_k, xa_v = _ctx_cache(cache_a_k), _ctx_cache(cache_a_v)
    xb_k, xb_v = _ctx_cache(cache_b_k), _ctx_cache(cache_b_v)
    xc_k, xc_v = _ctx_cache(cache_c_k), _ctx_cache(cache_c_v)

    tq_ctx = min(256, s_ctx)
    tq_dec = 256
    kv_layers = []
    for l in range(depth):
        lam0 = _lambda_init(l)
        mod_l = mod[l]
        proj, kv = _inproj(x, mod_l, w_in_p[l], rope_tab, gains[l], tm=tm, t_ctx=t_ctx, s_dec=s_dec)
        kv_layers.append(kv)

        common_a = dict(mode="diff", q_tile=QA_T, k_tile=KA_T, v_tile=VA_T, a_lambda=a_lambda[l],
                        subln_g=subln[l], lam_init=lam0, out_width=A_HEADS * HEAD_DIM)
        oa = _attn_full(proj, tok_off=0, n_batch=n_ctx_b, seq=s_ctx, tq=tq_ctx, name="attn_a_ctx", **common_a)
        oa = _attn_full(proj, tok_off=t_ctx, n_batch=n_dec_b, seq=s_dec, tq=tq_dec, ctx_k=xa_k[l],
                        ctx_v=xa_v[l], out_prev=oa, name="attn_a_dec", **common_a)

        common_b = dict(mode="gqa", q_tile=QB_T, k_tile=KB_T, v_tile=VB_T, out_width=B_HEADS * HEAD_DIM)
        ob = _attn_full(proj, tok_off=0, n_batch=n_ctx_b, seq=s_ctx, tq=tq_ctx // 2, name="attn_b_ctx", **common_b)
        ob = _attn_full(proj, tok_off=t_ctx, n_batch=n_dec_b, seq=s_dec, tq=tq_dec // 2, ctx_k=xb_k[l],
                        ctx_v=xb_v[l], out_prev=ob, name="attn_b_dec", **common_b)

        sink_p = c_sink[l]
        oc = _attn_full(proj, mode="gqa", tok_off=0, n_batch=n_ctx_b, seq=s_ctx, tq=tq_ctx // 2,
                        q_tile=QC_T, k_tile=KC_T, v_tile=VC_T, sink=sink_p,
                        out_width=C_HEADS * HEAD_DIM, name="attn_c_ctx")
        oc = _attn_window(proj, xc_k[l], xc_v[l], sink_p, oc, tok_off=t_ctx, n_batch=n_dec_b, seq=s_dec)

        x1, h2, gates, idx = _outproj(x, oa, ob, oc, wa[l], wb[l], wc[l], mod_l, ln1[l], wr[l], br[l],
                                      tm=tm, t_ctx=t_ctx, s_dec=s_dec, alpha=alpha)
        block_e, row_src, row_dst, n_used, n_out_rows, prime_row0 = _routing(idx[:, :TOP_K])
        y4 = _moe(h2, block_e, row_src, row_dst, n_used, w_gate_up, b_gate_up, w_down, b_down,
                  layer=l, n_out_rows=n_out_rows, prime_row0=prime_row0)
        x = _combine(x1, gates, y4, mod_l, ln2[l], tm=tm, t_ctx=t_ctx, s_dec=s_dec, alpha=alpha)

    y = x[:t_ctx].reshape(n_ctx_b, s_ctx, d)
    z = x[t_ctx:].reshape(n_dec_b, s_dec, d)
    kv_all = jnp.stack(kv_layers, axis=1).reshape(n_ctx_b, s_ctx, depth, 8 * LANE)
    kv_all = jnp.transpose(kv_all, (0, 2, 1, 3))

    def cache_out(lo, width, heads):
        return kv_all[..., lo:lo + width].reshape(n_ctx_b, depth, s_ctx, heads, width // heads)

    new_a_k = cache_out(0, 256, A_HEADS)
    new_b_k = cache_out(256, 128, B_KV_HEADS)
    new_c_k = cache_out(384, 128, C_KV_HEADS)
    new_a_v = cache_out(512, 256, A_HEADS)
    new_b_v = cache_out(768, 128, B_KV_HEADS)
    new_c_v = cache_out(896, 128, C_KV_HEADS)
    return (y, z, new_a_k, new_a_v, new_b_k, new_b_v, new_c_k, new_c_v)
```
